```python
import math
import jax, jax.numpy as jnp
from jax import lax
import numpy as np

D_MODEL = 1024
BATCH = 16
SEQ = 2048
DEPTH = 1
DEC_BATCH = 128
DEC_SEQ = 8
PAST_LEN = 8192
PAGE_SIZE = 128

DA_HEADS = 8
DA_KV_HEADS = 4
DA_GROUP = DA_HEADS // DA_KV_HEADS
DA_HEAD_DIM = 64
DA_V_DIM = 2 * DA_HEAD_DIM
DA_WIDTH = DA_HEADS * DA_V_DIM
MLA_HEADS = 8
MLA_Q_RANK = 384
MLA_KV_RANK = 256
MLA_NOPE = 64
MLA_ROPE = 32
MLA_QK_DIM = MLA_NOPE + MLA_ROPE
MLA_V_DIM = 64
MLA_WIDTH = MLA_HEADS * MLA_V_DIM
ROPE_THETA = 10000.0
REL_BUCKETS = 32
REL_MAX_DIST = 128
Q_BLOCK = 128
EPS = 1e-6
IN_SPLITS = (DA_HEADS * 2 * DA_HEAD_DIM, DA_KV_HEADS * 2 * DA_HEAD_DIM, DA_KV_HEADS * DA_V_DIM, DA_WIDTH,
             MLA_Q_RANK, MLA_KV_RANK, MLA_ROPE, MLA_WIDTH, D_MODEL, D_MODEL)
IN_COLS = sum(IN_SPLITS)

kernel_name = 'hybrid_diffattn_mla_gated_decode_step'


def rms_norm(x, g):
    xf = x.astype(jnp.float32)
    y = xf * lax.rsqrt(jnp.mean(xf * xf, axis=-1, keepdims=True) + EPS)
    return (y * g.astype(jnp.float32)).astype(x.dtype)


def rope(x, pos):
    half = MLA_ROPE // 2
    inv = jnp.power(ROPE_THETA, -jnp.arange(half, dtype=jnp.float32) / half)
    ang = pos.astype(jnp.float32)[:, None] * inv[None, :]
    cos = jnp.cos(ang)[:, None, :].astype(x.dtype)
    sin = jnp.sin(ang)[:, None, :].astype(x.dtype)
    x1, x2 = x[..., :half], x[..., half:]
    return jnp.concatenate([x1 * cos - x2 * sin, x1 * sin + x2 * cos], axis=-1)


def rel_bucket(dist):
    n = jnp.maximum(dist, 0)
    max_exact = REL_BUCKETS // 2
    nf = jnp.maximum(n, 1).astype(jnp.float32)
    large = max_exact + (jnp.log(nf / max_exact) / math.log(REL_MAX_DIST / max_exact)
                         * (REL_BUCKETS - max_exact)).astype(jnp.int32)
    large = jnp.minimum(large, REL_BUCKETS - 1)
    return jnp.where(n < max_exact, n, large)


def t5_bias(rel_bias, q_pos, k_pos):
    b = rel_bias[rel_bucket(q_pos[:, None] - k_pos[None, :])]
    b = jnp.transpose(b, (2, 0, 1)).astype(jnp.float32)
    return b.reshape(DA_KV_HEADS, DA_GROUP, q_pos.shape[0], k_pos.shape[0])


def _split_in(proj):
    idx = [int(i) for i in np.cumsum(IN_SPLITS)[:-1]]
    return jnp.split(proj, idx, axis=-1)


def _token_projections(x, pos, p):
    B, T, _ = x.shape
    h = rms_norm(x, p['norm_g'])
    qa, ka, va, za, cq, ckv, kr, zb, ga, gb = _split_in(h @ p['w_in'])
    qa = rms_norm(qa.reshape(B, T, DA_KV_HEADS, DA_GROUP, 2, DA_HEAD_DIM), p['da_q_norm'])
    ka = rms_norm(ka.reshape(B, T, DA_KV_HEADS, 2, DA_HEAD_DIM), p['da_k_norm'])
    ka = ka.reshape(B, T, DA_KV_HEADS, 2 * DA_HEAD_DIM)
    va = va.reshape(B, T, DA_KV_HEADS, DA_V_DIM)
    cq = rms_norm(cq, p['mla_cq_norm'])
    qm = rms_norm((cq @ p['w_uq']).reshape(B, T, MLA_HEADS, MLA_QK_DIM), p['mla_q_norm'])
    qm = jnp.concatenate([qm[..., :MLA_NOPE], rope(qm[..., MLA_NOPE:], pos)], axis=-1)
    ckv = rms_norm(ckv, p['mla_ckv_norm'])
    return dict(q1=qa[..., 0, :], q2=qa[..., 1, :], k=ka, v=va, za=za, qm=qm,
                c=ckv, kr=kr, zb=zb, ga=ga, gb=gb)


def mla_keys_values(c, kr, pos, p):
    kv = (c @ p['w_ukv']).reshape(c.shape[:-1] + (MLA_HEADS, MLA_NOPE + MLA_V_DIM))
    k_nope, v = kv[..., :MLA_NOPE], kv[..., MLA_NOPE:]
    k = jnp.concatenate([k_nope, jnp.broadcast_to(kr[..., None, :], k_nope.shape[:-1] + (MLA_ROPE,))], axis=-1)
    k = rms_norm(k, p['mla_k_norm'])
    k = jnp.concatenate([k[..., :MLA_NOPE], rope(k[..., MLA_NOPE:], pos)], axis=-1)
    return k, v


def diff_core(q1, q2, k1, k2, v, bias, mask, lam):
    scale = DA_HEAD_DIM ** -0.5
    def probs(q, k):
        s = jnp.einsum('qhgd,khd->hgqk', q, k).astype(jnp.float32) * scale + bias
        return jax.nn.softmax(jnp.where(mask, s, -jnp.inf), axis=-1)
    attn = probs(q1, k1) - lam * probs(q2, k2)
    o = jnp.einsum('hgqk,khe->qhge', attn.astype(v.dtype), v)
    return o.reshape(o.shape[0], DA_HEADS, DA_V_DIM)


def mla_core(q, k, v, mask):
    s = jnp.einsum('qhd,khd->hqk', q, k).astype(jnp.float32) * (MLA_QK_DIM ** -0.5)
    pr = jax.nn.softmax(jnp.where(mask, s, -jnp.inf), axis=-1)
    return jnp.einsum('hqk,khe->qhe', pr.astype(v.dtype), v)


def _prompt_attention(t, pos, rel_bias, lam, p):
    B, S = t['c'].shape[:2]
    km, vm = mla_keys_values(t['c'], t['kr'], pos, p)
    k1, k2 = t['k'][..., :DA_HEAD_DIM], t['k'][..., DA_HEAD_DIM:]
    diff_b = jax.vmap(diff_core, in_axes=(0, 0, 0, 0, 0, None, None, None))
    mla_b = jax.vmap(mla_core, in_axes=(0, 0, 0, None))
    def block(start):
        qp = start + jnp.arange(Q_BLOCK, dtype=jnp.int32)
        mask = qp[:, None] >= pos[None, :]
        bias = t5_bias(rel_bias, qp, pos)
        sl = lambda a: lax.dynamic_slice_in_dim(a, start, Q_BLOCK, axis=1)
        oa = diff_b(sl(t['q1']), sl(t['q2']), k1, k2, t['v'], bias, mask, lam)
        ob = mla_b(sl(t['qm']), km, vm, mask)
        return oa, ob
    oa, ob = lax.map(block, jnp.arange(0, S, Q_BLOCK, dtype=jnp.int32))
    unblock = lambda o: jnp.swapaxes(o, 0, 1).reshape((B, S) + o.shape[3:])
    return unblock(oa), unblock(ob)


def _sample_attention(t, ck, cv, cc, cr, page_table, k_pos, mask, bias, lam, p):
    def gather(cache, pt, new):
        return jnp.concatenate([cache[pt].reshape((-1,) + cache.shape[2:]), new], axis=0)
    def per_seq(args):
        pt, q1, q2, k_new, v_new, qm, c_new, kr_new = args
        k_all = gather(ck, pt, k_new)
        v_all = gather(cv, pt, v_new)
        km, vm = mla_keys_values(gather(cc, pt, c_new), gather(cr, pt, kr_new), k_pos, p)
        oa = diff_core(q1, q2, k_all[..., :DA_HEAD_DIM], k_all[..., DA_HEAD_DIM:], v_all, bias, mask, lam)
        ob = mla_core(qm, km, vm, mask)
        return oa, ob
    return lax.map(per_seq, (page_table, t['q1'], t['q2'], t['k'], t['v'], t['qm'], t['c'], t['kr']))


def _merge(x, oa, ob, t, lam_init, p):
    B, T, _ = x.shape
    oa = rms_norm(oa, p['da_out_norm']) * (1.0 - lam_init)
    ya = (oa.reshape(B, T, DA_WIDTH) * jax.nn.silu(t['za'])) @ p['w_pa']
    yb = (ob.reshape(B, T, MLA_WIDTH) * jax.nn.silu(t['zb'])) @ p['w_pb']
    m = jax.nn.sigmoid(t['ga']) * ya + jax.nn.sigmoid(t['gb']) * yb
    return x + m @ p['w_o']


def setup_inputs(seed: int = 0) -> dict:
    key = jax.random.key(seed)
    ks = jax.random.split(key, 32)
    f32 = jnp.float32
    n_pages = PAST_LEN // PAGE_SIZE
    n_used = DEC_BATCH * n_pages
    pool = (5 * n_used + 3) // 4
    nrm = lambda k, shape, s: jax.random.normal(k, shape, f32) * s
    gain = lambda k, shape: 1.0 + 0.02 * jax.random.normal(k, shape, f32)
    page_table = jax.random.permutation(ks[6], pool)[:n_used].reshape(DEC_BATCH, n_pages).astype(jnp.int32)
    return {
        'x_prompt': nrm(ks[0], (BATCH, SEQ, D_MODEL), 1.0),
        'x_sample': nrm(ks[1], (DEC_BATCH, DEC_SEQ, D_MODEL), 1.0),
        'cache_diff_k': nrm(ks[2], (DEPTH, pool, PAGE_SIZE, DA_KV_HEADS, 2 * DA_HEAD_DIM), 1.0),
        'cache_diff_v': nrm(ks[3], (DEPTH, pool, PAGE_SIZE, DA_KV_HEADS, DA_V_DIM), 1.0),
        'cache_mla_latent': nrm(ks[4], (DEPTH, pool, PAGE_SIZE, MLA_KV_RANK), 1.0),
        'cache_mla_krope': nrm(ks[5], (DEPTH, pool, PAGE_SIZE, MLA_ROPE), 1.0),
        'page_table': page_table,
        'rel_bias': nrm(ks[7], (REL_BUCKETS, DA_HEADS), 0.5),
        'norm_g': gain(ks[8], (DEPTH, D_MODEL)),
        'w_in': nrm(ks[9], (DEPTH, D_MODEL, IN_COLS), D_MODEL ** -0.5),
        'da_q_norm': gain(ks[10], (DEPTH, 2, DA_HEAD_DIM)),
        'da_k_norm': gain(ks[11], (DEPTH, 2, DA_HEAD_DIM)),
        'lam_q1': nrm(ks[12], (DEPTH, DA_HEAD_DIM), 0.1),
        'lam_k1': nrm(ks[13], (DEPTH, DA_HEAD_DIM), 0.1),
        'lam_q2': nrm(ks[14], (DEPTH, DA_HEAD_DIM), 0.1),
        'lam_k2': nrm(ks[15], (DEPTH, DA_HEAD_DIM), 0.1),
        'da_out_norm': gain(ks[16], (DEPTH, DA_V_DIM)),
        'w_pa': nrm(ks[17], (DEPTH, DA_WIDTH, D_MODEL), DA_WIDTH ** -0.5),
        'mla_cq_norm': gain(ks[18], (DEPTH, MLA_Q_RANK)),
        'w_uq': nrm(ks[19], (DEPTH, MLA_Q_RANK, MLA_HEADS * MLA_QK_DIM), MLA_Q_RANK ** -0.5),
        'mla_ckv_norm': gain(ks[20], (DEPTH, MLA_KV_RANK)),
        'w_ukv': nrm(ks[21], (DEPTH, MLA_KV_RANK, MLA_HEADS * (MLA_NOPE + MLA_V_DIM)), MLA_KV_RANK ** -0.5),
        'mla_q_norm': gain(ks[22], (DEPTH, MLA_QK_DIM)),
        'mla_k_norm': gain(ks[23], (DEPTH, MLA_QK_DIM)),
        'w_pb': nrm(ks[24], (DEPTH, MLA_WIDTH, D_MODEL), MLA_WIDTH ** -0.5),
        'w_o': nrm(ks[25], (DEPTH, D_MODEL, D_MODEL), D_MODEL ** -0.5),
    }


def reference(x_prompt, x_sample, cache_diff_k, cache_diff_v, cache_mla_latent, cache_mla_krope, page_table,
              rel_bias, norm_g, w_in, da_q_norm, da_k_norm, lam_q1, lam_k1, lam_q2, lam_k2, da_out_norm, w_pa,
              mla_cq_norm, w_uq, mla_ckv_norm, w_ukv, mla_q_norm, mla_k_norm, w_pb, w_o):
    S = x_prompt.shape[1]
    T = x_sample.shape[1]
    past = page_table.shape[1] * cache_diff_k.shape[2]
    pos_p = jnp.arange(S, dtype=jnp.int32)
    pos_s = past + jnp.arange(T, dtype=jnp.int32)
    k_pos_s = jnp.arange(past + T, dtype=jnp.int32)
    mask_s = pos_s[:, None] >= k_pos_s[None, :]
    bias_s = t5_bias(rel_bias, pos_s, k_pos_s)
    xp, xs = x_prompt, x_sample
    nk_p, nv_p, nc_p, nr_p, nk_s, nv_s, nc_s, nr_s = [], [], [], [], [], [], [], []
    for l in range(DEPTH):
        p = dict(norm_g=norm_g[l], w_in=w_in[l], da_q_norm=da_q_norm[l], da_k_norm=da_k_norm[l],
                 da_out_norm=da_out_norm[l], w_pa=w_pa[l], mla_cq_norm=mla_cq_norm[l], w_uq=w_uq[l],
                 mla_ckv_norm=mla_ckv_norm[l], w_ukv=w_ukv[l], mla_q_norm=mla_q_norm[l],
                 mla_k_norm=mla_k_norm[l], w_pb=w_pb[l], w_o=w_o[l])
        lam_init = 0.8 - 0.6 * math.exp(-0.3 * l)
        lam = (jnp.exp(jnp.sum(lam_q1[l] * lam_k1[l]).astype(jnp.float32))
               - jnp.exp(jnp.sum(lam_q2[l] * lam_k2[l]).astype(jnp.float32)) + lam_init)
        tp = _token_projections(xp, pos_p, p)
        oa_p, ob_p = _prompt_attention(tp, pos_p, rel_bias, lam, p)
        ts = _token_projections(xs, pos_s, p)
        oa_s, ob_s = _sample_attention(ts, cache_diff_k[l], cache_diff_v[l], cache_mla_latent[l],
                                       cache_mla_krope[l], page_table, k_pos_s, mask_s, bias_s, lam, p)
        xp = _merge(xp, oa_p, ob_p, tp, lam_init, p)
        xs = _merge(xs, oa_s, ob_s, ts, lam_init, p)
        nk_p.append(tp['k']); nv_p.append(tp['v']); nc_p.append(tp['c']); nr_p.append(tp['kr'])
        nk_s.append(ts['k']); nv_s.append(ts['v']); nc_s.append(ts['c']); nr_s.append(ts['kr'])
    return (xp, xs, jnp.stack(nk_p), jnp.stack(nv_p), jnp.stack(nc_p), jnp.stack(nr_p),
            jnp.stack(nk_s), jnp.stack(nv_s), jnp.stack(nc_s), jnp.stack(nr_s))
```

```python
import functools
import math

import numpy as np
import jax
import jax.numpy as jnp
from jax import lax
from jax.experimental import pallas as pl
from jax.experimental.pallas import tpu as pltpu

F32 = jnp.float32
BF16 = jnp.bfloat16

D_MODEL = 1024
DA_HEADS = 8
DA_KV_HEADS = 4
DA_GROUP = DA_HEADS // DA_KV_HEADS
DA_HEAD_DIM = 64
DA_V_DIM = 2 * DA_HEAD_DIM
DA_WIDTH = DA_HEADS * DA_V_DIM
MLA_HEADS = 8
MLA_Q_RANK = 384
MLA_KV_RANK = 256
MLA_NOPE = 64
MLA_ROPE = 32
MLA_QK_DIM = MLA_NOPE + MLA_ROPE
MLA_V_DIM = 64
MLA_WIDTH = MLA_HEADS * MLA_V_DIM
ROPE_THETA = 10000.0
REL_BUCKETS = 32
REL_MAX_DIST = 128
EPS = 1e-6
IN_SPLITS = (DA_HEADS * 2 * DA_HEAD_DIM, DA_KV_HEADS * 2 * DA_HEAD_DIM, DA_KV_HEADS * DA_V_DIM, DA_WIDTH,
             MLA_Q_RANK, MLA_KV_RANK, MLA_ROPE, MLA_WIDTH, D_MODEL, D_MODEL)

LANES = 128
VMEM_LIMIT_BYTES = 56 * 1024 * 1024

MASKED = -1e30
HALF = MLA_ROPE // 2
ROPE_LO = MLA_NOPE

_Q0, _K0, _V0, _CQ0, _CKV0, _KR0, _MAIN_COLS = 0, 1024, 1536, 2048, 2432, 2688, 2816

TOKEN_TILE = 512
ATT_TILE = 512
PAGES_PER_STEP = 8


def _const_spec(shape):
    nd = len(shape)
    return pl.BlockSpec(shape, lambda *_: (0,) * nd, pipeline_mode=pl.Buffered(1))


def _dot(a, b):
    return jnp.dot(a, b, preferred_element_type=F32)


def _dot_nt(a, b):
    return lax.dot_general(a, b, (((1,), (1,)), ((), ())), preferred_element_type=F32)


def _dot_tn(a, b):
    return lax.dot_general(a, b, (((0,), (0,)), ((), ())), preferred_element_type=F32)


def _lane_iota():
    return lax.broadcasted_iota(jnp.int32, (1, LANES), 1)


def _rope128(y, cos_t, sin_t):
    lane = _lane_iota()
    swapped = jnp.where(lane < ROPE_LO + HALF, pltpu.roll(y, LANES - HALF, 1), pltpu.roll(y, HALF, 1))
    return y * cos_t + swapped * sin_t


def _project_body(with_kv, x_ref, ng_ref, wm_ref, gq_ref, gk_ref, gcq_ref, wuq_ref, gmq_ref, gckv_ref,
                  wuk_ref, wuv_ref, gmk_ref, cos_ref, sin_ref,
                  qd_ref, kd_ref, kdb_ref, vd_ref, vdb_ref, c_ref, kr_ref, qm_ref, *kv_refs):
    x = x_ref[...]
    h = x * lax.rsqrt(jnp.mean(x * x, axis=-1, keepdims=True) + EPS) * ng_ref[...]
    hb = h.astype(BF16)
    lane = _lane_iota()
    lo = lane < DA_HEAD_DIM
    cos_t = cos_ref[...]
    sin_t = sin_ref[...]

    def pair_norm(blk, g):
        sq = blk * blk
        s_lo = jnp.sum(jnp.where(lo, sq, 0.0), axis=-1, keepdims=True)
        s_hi = jnp.sum(jnp.where(lo, 0.0, sq), axis=-1, keepdims=True)
        r = jnp.where(lo, lax.rsqrt(s_lo / DA_HEAD_DIM + EPS), lax.rsqrt(s_hi / DA_HEAD_DIM + EPS))
        return blk * r * g

    def head_norm(blk, g):
        ss = jnp.sum(blk * blk, axis=-1, keepdims=True)
        return blk * lax.rsqrt(ss / MLA_QK_DIM + EPS) * g

    qa = _dot(hb, wm_ref[:, _Q0:_K0])
    gq = gq_ref[...]
    for hq in range(DA_HEADS):
        sl = slice(hq * LANES, (hq + 1) * LANES)
        qd_ref[:, sl] = pair_norm(qa[:, sl], gq).astype(BF16)
    ka = _dot(hb, wm_ref[:, _K0:_V0])
    gk = gk_ref[...]
    for hk in range(DA_KV_HEADS):
        sl = slice(hk * LANES, (hk + 1) * LANES)
        kn = pair_norm(ka[:, sl], gk)
        kd_ref[:, sl] = kn
        kdb_ref[:, sl] = kn.astype(BF16)
    va = _dot(hb, wm_ref[:, _V0:_CQ0])
    vd_ref[...] = va
    vdb_ref[...] = va.astype(BF16)

    cq = _dot(hb, wm_ref[:, _CQ0:_CKV0])
    cq = cq * lax.rsqrt(jnp.mean(cq * cq, axis=-1, keepdims=True) + EPS) * gcq_ref[...]
    qm = _dot(cq.astype(BF16), wuq_ref[...])
    gmq = gmq_ref[...]
    for hm in range(MLA_HEADS):
        sl = slice(hm * LANES, (hm + 1) * LANES)
        qm_ref[:, sl] = _rope128(head_norm(qm[:, sl], gmq), cos_t, sin_t).astype(BF16)

    ckv = _dot(hb, wm_ref[:, _CKV0:_KR0])
    c = ckv * lax.rsqrt(jnp.mean(ckv * ckv, axis=-1, keepdims=True) + EPS) * gckv_ref[...]
    c_ref[...] = c
    krp = _dot(hb, wm_ref[:, _KR0:_MAIN_COLS])
    kr_ref[...] = krp[:, ROPE_LO:ROPE_LO + MLA_ROPE]

    if with_kv:
        km_ref, vm_ref = kv_refs
        cb = c.astype(BF16)
        kn = _dot(cb, wuk_ref[...])
        gmk = gmk_ref[...]
        for hm in range(MLA_HEADS):
            sl = slice(hm * LANES, (hm + 1) * LANES)
            km_ref[:, sl] = _rope128(head_norm(kn[:, sl] + krp, gmk), cos_t, sin_t).astype(BF16)
        vm_ref[...] = _dot(cb, wuv_ref[...]).astype(BF16)


def _project(x2d, w, cos_tab, sin_tab, with_kv):
    n = x2d.shape[0]
    tm = min(TOKEN_TILE, n)
    assert n % tm == 0 and cos_tab.shape[0] % tm == 0
    n_pos = cos_tab.shape[0] // tm
    row = lambda cols: pl.BlockSpec((tm, cols), lambda i: (i, 0))
    pos = pl.BlockSpec((tm, LANES), lambda i: (i % n_pos, 0))
    consts = [w['norm_g'], w['w_main'], w['gq'], w['gk'], w['gcq'], w['w_uq'], w['gmq'], w['gckv'],
              w['w_uk_pad'], w['w_uv'], w['gmk']]
    out_shape = [jax.ShapeDtypeStruct((n, DA_WIDTH), BF16),
                 jax.ShapeDtypeStruct((n, DA_KV_HEADS * LANES), F32),
                 jax.ShapeDtypeStruct((n, DA_KV_HEADS * LANES), BF16),
                 jax.ShapeDtypeStruct((n, DA_KV_HEADS * DA_V_DIM), F32),
                 jax.ShapeDtypeStruct((n, DA_KV_HEADS * DA_V_DIM), BF16),
                 jax.ShapeDtypeStruct((n, MLA_KV_RANK), F32),
                 jax.ShapeDtypeStruct((n, MLA_ROPE), F32),
                 jax.ShapeDtypeStruct((n, MLA_HEADS * LANES), BF16)]
    out_specs = [row(DA_WIDTH), row(512), row(512), row(512), row(512), row(MLA_KV_RANK), row(MLA_ROPE),
                 row(MLA_HEADS * LANES)]
    if with_kv:
        out_shape += [jax.ShapeDtypeStruct((n, MLA_HEADS * LANES), BF16),
                      jax.ShapeDtypeStruct((n, MLA_WIDTH), BF16)]
        out_specs += [row(MLA_HEADS * LANES), row(MLA_WIDTH)]
    return pl.pallas_call(
        functools.partial(_project_body, with_kv),
        grid=(n // tm,),
        in_specs=[row(D_MODEL)] + [_const_spec(a.shape) for a in consts] + [pos, pos],
        out_specs=out_specs,
        out_shape=out_shape,
        compiler_params=pltpu.CompilerParams(dimension_semantics=("parallel",), vmem_limit_bytes=VMEM_LIMIT_BYTES),
        name="project_kv" if with_kv else "project",
    )(x2d, *consts, cos_tab, sin_tab)


def _online_softmax_step(s, v, m_ref, l_ref, acc_ref):
    m_old = m_ref[...]
    m_new = jnp.maximum(m_old, jnp.max(s, axis=-1, keepdims=True))
    alpha = jnp.exp(m_old - m_new)
    p = jnp.exp(s - m_new)
    l_ref[...] = alpha * l_ref[...] + jnp.sum(p, axis=-1, keepdims=True)
    acc_ref[...] = alpha * acc_ref[...] + _dot(p.astype(BF16), v)
    m_ref[...] = m_new


def _diff_prompt_body(lam_ref, q_ref, k_ref, v_ref, bdiag_ref, bsub_ref, o_ref, qs_ref, m_ref, l_ref, acc_ref):
    i = pl.program_id(2)
    t = ATT_TILE
    lane = _lane_iota()
    lo = lane < DA_HEAD_DIM
    lam = lam_ref[0]
    corner = LANES

    for m in range(2):
        for g in range(DA_GROUP):
            qg = q_ref[:, g * LANES:(g + 1) * LANES]
            keep = lo if m == 0 else jnp.logical_not(lo)
            qs_ref[g * t:(g + 1) * t, :] = jnp.where(keep, qg, jnp.zeros_like(qg))
        m_ref[...] = jnp.full(m_ref.shape, MASKED, F32)
        l_ref[...] = jnp.zeros(l_ref.shape, F32)
        acc_ref[...] = jnp.zeros(acc_ref.shape, F32)

        def far(j, carry):
            ks = pl.multiple_of(j * t, t)
            s = _dot_nt(qs_ref[...], k_ref[pl.ds(ks, t), :])
            _online_softmax_step(s, v_ref[pl.ds(ks, t), :], m_ref, l_ref, acc_ref)
            return carry
        lax.fori_loop(0, jnp.maximum(i - 1, 0), far, 0)

        @pl.when(i >= 1)
        def _():
            ks = pl.multiple_of((i - 1) * t, t)
            s = _dot_nt(qs_ref[...], k_ref[pl.ds(ks, t), :])
            pieces = []
            for g in range(DA_GROUP):
                top = s[g * t:g * t + corner, :]
                top = jnp.concatenate([top[:, :t - corner], top[:, t - corner:] + bsub_ref[g]], axis=1)
                pieces += [top, s[g * t + corner:(g + 1) * t, :]]
            s = jnp.concatenate(pieces, axis=0)
            _online_softmax_step(s, v_ref[pl.ds(ks, t), :], m_ref, l_ref, acc_ref)

        ks = pl.multiple_of(i * t, t)
        s = _dot_nt(qs_ref[...], k_ref[pl.ds(ks, t), :])
        s = jnp.concatenate([s[g * t:(g + 1) * t, :] + bdiag_ref[g] for g in range(DA_GROUP)], axis=0)
        _online_softmax_step(s, v_ref[pl.ds(ks, t), :], m_ref, l_ref, acc_ref)

        o = acc_ref[...] * (1.0 / l_ref[...])
        for g in range(DA_GROUP):
            sl = slice(g * LANES, (g + 1) * LANES)
            if m == 0:
                o_ref[:, sl] = o[g * t:(g + 1) * t, :]
            else:
                o_ref[:, sl] = o_ref[:, sl] - lam * o[g * t:(g + 1) * t, :]


def _diff_prompt(lam, qd, kdb, vdb, bias_diag, bias_sub):
    b, s, _ = qd.shape
    t = ATT_TILE
    assert s % t == 0
    return pl.pallas_call(
        _diff_prompt_body,
        grid=(b, DA_KV_HEADS, s // t),
        in_specs=[pl.BlockSpec(memory_space=pltpu.SMEM),
                  pl.BlockSpec((None, t, DA_GROUP * LANES), lambda bi, h, i: (bi, i, h)),
                  pl.BlockSpec((None, s, LANES), lambda bi, h, i: (bi, 0, h)),
                  pl.BlockSpec((None, s, LANES), lambda bi, h, i: (bi, 0, h)),
                  pl.BlockSpec((DA_GROUP, t, t), lambda bi, h, i: (h, 0, 0)),
                  pl.BlockSpec((DA_GROUP, LANES, LANES), lambda bi, h, i: (h, 0, 0))],
        out_specs=pl.BlockSpec((None, t, DA_GROUP * LANES), lambda bi, h, i: (bi, i, h)),
        out_shape=jax.ShapeDtypeStruct((b, s, DA_WIDTH), F32),
        scratch_shapes=[pltpu.VMEM((DA_GROUP * t, LANES), BF16),
                        pltpu.VMEM((DA_GROUP * t, 1), F32),
                        pltpu.VMEM((DA_GROUP * t, 1), F32),
                        pltpu.VMEM((DA_GROUP * t, LANES), F32)],
        compiler_params=pltpu.CompilerParams(dimension_semantics=("parallel", "parallel", "arbitrary"),
                                             vmem_limit_bytes=VMEM_LIMIT_BYTES),
        name="diff_prompt",
    )(lam, qd, kdb, vdb, bias_diag, bias_sub)


def _mla_prompt_body(q_ref, k_ref, v_ref, o_ref, m_ref, l_ref, acc_ref):
    i = pl.program_id(2)
    t = ATT_TILE
    scale = MLA_QK_DIM ** -0.5
    lane = _lane_iota()
    causal = (lax.broadcasted_iota(jnp.int32, (t, t), 0) >= lax.broadcasted_iota(jnp.int32, (t, t), 1))
    outs = []
    for hh in range(2):
        sl = slice(hh * LANES, (hh + 1) * LANES)
        q = q_ref[:, sl]
        m_ref[...] = jnp.full(m_ref.shape, MASKED, F32)
        l_ref[...] = jnp.zeros(l_ref.shape, F32)
        acc_ref[...] = jnp.zeros(acc_ref.shape, F32)

        def far(j, carry):
            ks = pl.multiple_of(j * t, t)
            s = _dot_nt(q, k_ref[pl.ds(ks, t), sl]) * scale
            _online_softmax_step(s, v_ref[pl.ds(ks, t), :], m_ref, l_ref, acc_ref)
            return carry
        lax.fori_loop(0, i, far, 0)

        ks = pl.multiple_of(i * t, t)
        s = _dot_nt(q, k_ref[pl.ds(ks, t), sl]) * scale
        s = jnp.where(causal, s, MASKED)
        _online_softmax_step(s, v_ref[pl.ds(ks, t), :], m_ref, l_ref, acc_ref)
        outs.append(acc_ref[...] * (1.0 / l_ref[...]))
    o_ref[...] = jnp.where(lane < MLA_V_DIM, outs[0], outs[1])


def _mla_prompt(qm, km, vm):
    b, s, _ = qm.shape
    t = ATT_TILE
    return pl.pallas_call(
        _mla_prompt_body,
        grid=(b, MLA_HEADS // 2, s // t),
        in_specs=[pl.BlockSpec((None, t, 2 * LANES), lambda bi, h, i: (bi, i, h)),
                  pl.BlockSpec((None, s, 2 * LANES), lambda bi, h, i: (bi, 0, h)),
                  pl.BlockSpec((None, s, LANES), lambda bi, h, i: (bi, 0, h))],
        out_specs=pl.BlockSpec((None, t, LANES), lambda bi, h, i: (bi, i, h)),
        out_shape=jax.ShapeDtypeStruct((b, s, MLA_WIDTH), F32),
        scratch_shapes=[pltpu.VMEM((t, 1), F32), pltpu.VMEM((t, 1), F32), pltpu.VMEM((t, LANES), F32)],
        compiler_params=pltpu.CompilerParams(dimension_semantics=("parallel", "parallel", "arbitrary"),
                                             vmem_limit_bytes=VMEM_LIMIT_BYTES),
        name="mla_prompt",
    )(qm, km, vm)


def _sample_body(n_steps, pt_ref, lam_ref, *refs):
    pg = PAGES_PER_STEP
    kpages = refs[0:pg]
    vpages = refs[pg:2 * pg]
    cpages = refs[2 * pg:3 * pg]
    rpages = refs[3 * pg:4 * pg]
    (kn_ref, vn_ref, cn_ref, rn_ref, qbd_ref, qmn_ref, qrr_ref, gkn_ref, wuk_ref, wuv_ref, eseg_ref, ones_ref,
     bias_ref, mmask_ref, ctab_ref, stab_ref,
     oa_ref, ob_ref,
     qabs_ref, md_ref, ld_ref, accd_ref, mm_ref, lm_ref, accm_ref) = refs[4 * pg:]
    j = pl.program_id(1)
    scale_m = MLA_QK_DIM ** -0.5
    page = cpages[0].shape[0]

    def heads_to_lanes(ref2):
        t = ref2.shape[0] // DA_KV_HEADS
        return jnp.concatenate([ref2[pl.ds(h, t, stride=DA_KV_HEADS), :].astype(BF16) for h in range(DA_KV_HEADS)],
                               axis=1)

    @pl.when(j == 0)
    def _():
        md_ref[...] = jnp.full(md_ref.shape, MASKED, F32)
        ld_ref[...] = jnp.zeros(ld_ref.shape, F32)
        accd_ref[...] = jnp.zeros(accd_ref.shape, F32)
        mm_ref[...] = jnp.full(mm_ref.shape, MASKED, F32)
        lm_ref[...] = jnp.zeros(lm_ref.shape, F32)
        accm_ref[...] = jnp.zeros(accm_ref.shape, F32)
        qg = (qmn_ref[...].astype(F32) * gkn_ref[...]).astype(BF16)
        qabs_ref[...] = _dot(wuk_ref[...], qg).astype(BF16)

    def rescale_rows(alpha, n_rows):
        full = jnp.broadcast_to(alpha, (LANES, alpha.shape[1]))
        if alpha.shape[1] < LANES:
            full = jnp.concatenate([full, jnp.zeros((LANES, LANES - alpha.shape[1]), F32)], axis=1)
        return jnp.transpose(full)[:n_rows, :]

    def process(kb, vb, c, krt, bias, mmask, ctab, stab):
        cb = c.astype(BF16)
        s = _dot(kb, qbd_ref[...]) + bias
        m_old = md_ref[...]
        m_new = jnp.maximum(m_old, jnp.max(s, axis=0, keepdims=True))
        alpha = jnp.exp(m_old - m_new)
        p = jnp.exp(s - m_new)
        ld_ref[...] = alpha * ld_ref[...] + jnp.sum(p, axis=0, keepdims=True)
        md_ref[...] = m_new
        a_rows = rescale_rows(alpha, LANES)
        pv = _dot_tn(p.astype(BF16), vb)
        accd_ref[...] = jnp.concatenate([a_rows] * DA_KV_HEADS, axis=1) * accd_ref[...] + pv
        kn = _dot(cb, wuk_ref[...])
        ss = _dot((kn * kn).astype(BF16), eseg_ref[...])
        ss = ss + _dot_tn((krt * krt).astype(BF16), ones_ref[...])
        r = lax.rsqrt(ss / MLA_QK_DIM + EPS)
        raw = _dot(cb, qabs_ref[...])
        rot = jnp.concatenate([krt * ctab, krt * stab], axis=0).astype(BF16)
        raw = raw + _dot_tn(rot, qrr_ref[...])
        sm = raw * r * scale_m
        if mmask is not None:
            sm = sm + mmask
        m_old = mm_ref[...]
        m_new = jnp.maximum(m_old, jnp.max(sm, axis=0, keepdims=True))
        alpha = jnp.exp(m_old - m_new)
        p = jnp.exp(sm - m_new)
        lm_ref[...] = alpha * lm_ref[...] + jnp.sum(p, axis=0, keepdims=True)
        mm_ref[...] = m_new
        a_rows = rescale_rows(alpha, MLA_HEADS * 8)
        pc = _dot_tn(p.astype(BF16), cb)
        accm_ref[...] = jnp.concatenate([a_rows] * (MLA_KV_RANK // LANES), axis=1) * accm_ref[...] + pc

    base = pl.multiple_of(j * (pg * page), pg * page)
    kb = jnp.concatenate([heads_to_lanes(r) for r in kpages], axis=0)
    vb = jnp.concatenate([heads_to_lanes(r) for r in vpages], axis=0)
    c = jnp.concatenate([r[...] for r in cpages], axis=0)
    krt = jnp.concatenate([r[...] for r in rpages], axis=1)
    rows = pl.ds(base, pg * page)
    process(kb, vb, c, krt, bias_ref[rows, :], None, ctab_ref[:, rows], stab_ref[:, rows])

    @pl.when(j == n_steps - 1)
    def _():
        past = n_steps * pg * page
        t_new = kn_ref.shape[0]
        rows_n = pl.ds(past, t_new)
        process(kn_ref[...].astype(BF16), vn_ref[...].astype(BF16), cn_ref[...], rn_ref[...], bias_ref[rows_n, :],
                mmask_ref[...], ctab_ref[:, rows_n], stab_ref[:, rows_n])
        lam = lam_ref[0]
        inv_d = rescale_rows(1.0 / ld_ref[...], LANES)
        accd = accd_ref[...]
        lanes_per_kv = 2 * DA_GROUP * t_new
        for h in range(DA_KV_HEADS):
            cols = slice(h * LANES, (h + 1) * LANES)
            for g in range(DA_GROUP):
                r1 = h * lanes_per_kv + g * t_new
                r2 = r1 + DA_GROUP * t_new
                o1 = accd[r1:r1 + t_new, cols] * inv_d[r1:r1 + t_new, :]
                o2 = accd[r2:r2 + t_new, cols] * inv_d[r2:r2 + t_new, :]
                hq = h * DA_GROUP + g
                oa_ref[:, hq * LANES:(hq + 1) * LANES] = o1 - lam * o2
        inv_m = rescale_rows(1.0 / lm_ref[...], MLA_HEADS * t_new)
        cbar = (accm_ref[...] * jnp.concatenate([inv_m] * (MLA_KV_RANK // LANES), axis=1)).astype(BF16)
        full = _dot(cbar, wuv_ref[...])
        lane5 = lax.broadcasted_iota(jnp.int32, (1, MLA_WIDTH), 1)
        ob = jnp.zeros((t_new, MLA_WIDTH), F32)
        for h in range(MLA_HEADS):
            sel = (lane5 >= h * MLA_V_DIM) & (lane5 < (h + 1) * MLA_V_DIM)
            ob = ob + jnp.where(sel, full[h * t_new:(h + 1) * t_new, :], 0.0)
        ob_ref[...] = ob


def _sample_attention(lam, page_table, ck, cv, cc, crt, new, qbd, qmn, qrr, w, bias_tab, mla_mask, ctab, stab):
    n_seq, n_pages = page_table.shape
    pool, page = cc.shape[0], cc.shape[1]
    t_new = new['kd'].shape[1]
    pg = PAGES_PER_STEP
    assert n_pages % pg == 0
    n_steps = n_pages // pg

    def page_spec(*tail):
        zeros = (0,) * len(tail)
        return lambda p: pl.BlockSpec((None,) + tail, lambda b, j, pt: (pt[b, j * pg + p],) + zeros)

    def seq_spec(*tail):
        zeros = (0,) * len(tail)
        return pl.BlockSpec((None,) + tail, lambda b, j, pt: (b,) + zeros)

    def const(shape):
        nd = len(shape)
        return pl.BlockSpec(shape, lambda b, j, pt: (0,) * nd, pipeline_mode=pl.Buffered(1))

    in_specs = [pl.BlockSpec(memory_space=pltpu.SMEM)]
    for mk in (page_spec(page * DA_KV_HEADS, LANES), page_spec(page * DA_KV_HEADS, DA_V_DIM),
               page_spec(page, MLA_KV_RANK), page_spec(MLA_ROPE, page)):
        in_specs += [mk(p) for p in range(pg)]
    per_seq = [new['kd'], new['vd'], new['c'], new['krt'], qbd, qmn, qrr]
    in_specs += [seq_spec(*a.shape[1:]) for a in per_seq]
    consts = [w['gk_nope_col'], w['w_uk'], w['w_uv'], w['eseg'], w['ones_rope'], bias_tab, mla_mask, ctab, stab]
    in_specs += [const(a.shape) for a in consts]
    operands = [ck] * pg + [cv] * pg + [cc] * pg + [crt] * pg + per_seq + consts
    grid_spec = pltpu.PrefetchScalarGridSpec(
        num_scalar_prefetch=1,
        grid=(n_seq, n_steps),
        in_specs=in_specs,
        out_specs=[pl.BlockSpec((None, t_new, DA_WIDTH), lambda b, j, pt: (b, 0, 0)),
                   pl.BlockSpec((None, t_new, MLA_WIDTH), lambda b, j, pt: (b, 0, 0))],
        scratch_shapes=[pltpu.VMEM((MLA_KV_RANK, MLA_HEADS * t_new), BF16),
                        pltpu.VMEM((1, LANES), F32), pltpu.VMEM((1, LANES), F32),
                        pltpu.VMEM((LANES, DA_KV_HEADS * DA_V_DIM), F32),
                        pltpu.VMEM((1, MLA_HEADS * t_new), F32), pltpu.VMEM((1, MLA_HEADS * t_new), F32),
                        pltpu.VMEM((MLA_HEADS * t_new, MLA_KV_RANK), F32)])
    return pl.pallas_call(
        functools.partial(_sample_body, n_steps),
        grid_spec=grid_spec,
        out_shape=[jax.ShapeDtypeStruct((n_seq, t_new, DA_WIDTH), F32),
                   jax.ShapeDtypeStruct((n_seq, t_new, MLA_WIDTH), F32)],
        compiler_params=pltpu.CompilerParams(dimension_semantics=("parallel", "arbitrary"),
                                             vmem_limit_bytes=VMEM_LIMIT_BYTES),
        name="sample_attention",
    )(page_table, lam, *operands)


def _merge_body(one_minus_lam_init, x_ref, oa_ref, ob_ref, ng_ref, wzg_ref, gout_ref, wpa_ref, wpb_ref, wo_ref, y_ref):
    x = x_ref[...]
    h = x * lax.rsqrt(jnp.mean(x * x, axis=-1, keepdims=True) + EPS) * ng_ref[...]
    hb = h.astype(BF16)
    za = _dot(hb, wzg_ref[:, 0:DA_WIDTH])
    gout = gout_ref[...]
    parts = []
    for hq in range(DA_HEADS):
        sl = slice(hq * LANES, (hq + 1) * LANES)
        o = oa_ref[:, sl]
        o = o * lax.rsqrt(jnp.mean(o * o, axis=-1, keepdims=True) + EPS) * gout * one_minus_lam_init
        z = za[:, sl]
        parts.append((o * (z * jax.nn.sigmoid(z))).astype(BF16))
    ya = _dot(jnp.concatenate(parts, axis=1), wpa_ref[...])
    zb = _dot(hb, wzg_ref[:, DA_WIDTH:DA_WIDTH + MLA_WIDTH])
    yb = _dot((ob_ref[...] * (zb * jax.nn.sigmoid(zb))).astype(BF16), wpb_ref[...])
    g0 = DA_WIDTH + MLA_WIDTH
    ga = _dot(hb, wzg_ref[:, g0:g0 + D_MODEL])
    gb = _dot(hb, wzg_ref[:, g0 + D_MODEL:g0 + 2 * D_MODEL])
    mix = jax.nn.sigmoid(ga) * ya + jax.nn.sigmoid(gb) * yb
    y_ref[...] = x + _dot(mix.astype(BF16), wo_ref[...])


def _merge(x2d, oa, ob, w, lam_init):
    n = x2d.shape[0]
    tm = min(TOKEN_TILE, n)
    assert n % tm == 0
    row = lambda cols: pl.BlockSpec((tm, cols), lambda i: (i, 0))
    consts = [w['norm_g'], w['w_zg'], w['gout'], w['w_pa'], w['w_pb'], w['w_o']]
    return pl.pallas_call(
        functools.partial(_merge_body, 1.0 - lam_init),
        grid=(n // tm,),
        in_specs=[row(D_MODEL), row(DA_WIDTH), row(MLA_WIDTH)] + [_const_spec(a.shape) for a in consts],
        out_specs=row(D_MODEL),
        out_shape=jax.ShapeDtypeStruct((n, D_MODEL), F32),
        compiler_params=pltpu.CompilerParams(dimension_semantics=("parallel",), vmem_limit_bytes=VMEM_LIMIT_BYTES),
        name="merge",
    )(x2d, oa, ob, *consts)


def _rel_bucket(dist):
    n = jnp.maximum(dist, 0)
    max_exact = REL_BUCKETS // 2
    nf = jnp.maximum(n, 1).astype(F32)
    large = max_exact + (jnp.log(nf / max_exact) / math.log(REL_MAX_DIST / max_exact)
                         * (REL_BUCKETS - max_exact)).astype(jnp.int32)
    large = jnp.minimum(large, REL_BUCKETS - 1)
    return jnp.where(n < max_exact, n, large)


def _rope_angles(pos):
    inv = jnp.power(ROPE_THETA, -jnp.arange(HALF, dtype=F32) / HALF)
    ang = pos.astype(F32)[:, None] * inv[None, :]
    return jnp.cos(ang), jnp.sin(ang)


def _rope_tables128(pos):
    cos, sin = _rope_angles(pos)
    n = pos.shape[0]
    cos_t = jnp.concatenate([jnp.ones((n, MLA_NOPE), F32), cos, cos, jnp.zeros((n, LANES - MLA_QK_DIM), F32)], axis=1)
    sin_t = jnp.concatenate([jnp.zeros((n, MLA_NOPE), F32), -sin, sin, jnp.zeros((n, LANES - MLA_QK_DIM), F32)], axis=1)
    return cos_t, sin_t


def _pad_heads(a, live):
    pad = [(0, 0)] * (a.ndim - 1) + [(0, LANES - live)]
    a = jnp.pad(a, pad)
    return a.reshape(a.shape[:-2] + (a.shape[-2] * LANES,))


def _prep_layer(p, t_new):
    offs = [int(v) for v in np.cumsum((0,) + IN_SPLITS)]
    wq, wk, wv, wza, wcq, wckv, wkr, wzb, wga, wgb = [p['w_in'][:, offs[n]:offs[n + 1]] for n in range(10)]
    wkr_placed = jnp.pad(wkr, ((0, 0), (ROPE_LO, LANES - ROPE_LO - MLA_ROPE)))
    w_ukv = p['w_ukv'].reshape(MLA_KV_RANK, MLA_HEADS, MLA_NOPE + MLA_V_DIM)
    w_uk = w_ukv[:, :, :MLA_NOPE]
    gmk = p['mla_k_norm']
    head_of_col = np.arange(MLA_HEADS * MLA_NOPE) // MLA_NOPE
    head_of_lane = np.arange(MLA_HEADS * t_new) // t_new
    return dict(
        norm_g=p['norm_g'].reshape(1, D_MODEL),
        w_main=jnp.concatenate([wq, wk, wv, wcq, wckv, wkr_placed], axis=1).astype(BF16),
        w_zg=jnp.concatenate([wza, wzb, wga, wgb], axis=1).astype(BF16),
        gq=p['da_q_norm'].reshape(1, LANES) * (DA_HEAD_DIM ** -0.5),
        gk=p['da_k_norm'].reshape(1, LANES),
        gcq=p['mla_cq_norm'].reshape(1, MLA_Q_RANK),
        w_uq=_pad_heads(p['w_uq'].reshape(MLA_Q_RANK, MLA_HEADS, MLA_QK_DIM), MLA_QK_DIM).astype(BF16),
        gmq=jnp.pad(p['mla_q_norm'], (0, LANES - MLA_QK_DIM)).reshape(1, LANES),
        gckv=p['mla_ckv_norm'].reshape(1, MLA_KV_RANK),
        w_uk_pad=_pad_heads(w_uk, MLA_NOPE).astype(BF16),
        w_uk=w_uk.reshape(MLA_KV_RANK, MLA_HEADS * MLA_NOPE).astype(BF16),
        w_uv=w_ukv[:, :, MLA_NOPE:].reshape(MLA_KV_RANK, MLA_WIDTH).astype(BF16),
        gmk=jnp.pad(gmk, (0, LANES - MLA_QK_DIM)).reshape(1, LANES),
        gk_nope_col=jnp.tile(gmk[:MLA_NOPE], MLA_HEADS).reshape(MLA_HEADS * MLA_NOPE, 1),
        gk_rope=gmk[MLA_NOPE:],
        eseg=jnp.asarray(head_of_col[:, None] == head_of_lane[None, :], BF16),
        ones_rope=jnp.ones((MLA_ROPE, MLA_HEADS * t_new), BF16),
        gout=p['da_out_norm'].reshape(1, DA_V_DIM),
        w_pa=p['w_pa'].astype(BF16), w_pb=p['w_pb'].astype(BF16), w_o=p['w_o'].astype(BF16),
    )


def _prompt_bias_tiles(rel_bias):
    t = ATT_TILE
    far = rel_bias[REL_BUCKETS - 1]
    r = np.arange(t)[:, None]
    c = np.arange(t)[None, :]
    diag = rel_bias[_rel_bucket(jnp.asarray(r - c, jnp.int32))] - far
    diag = jnp.where((r >= c)[:, :, None], diag, MASKED)
    rs = np.arange(LANES)[:, None]
    cs = np.arange(LANES)[None, :] + (t - LANES)
    sub = rel_bias[_rel_bucket(jnp.asarray(t + rs - cs, jnp.int32))] - far
    return jnp.transpose(diag, (2, 0, 1)), jnp.transpose(sub, (2, 0, 1))


def _sample_tables(rel_bias, gk_rope, past, t_new):
    n_keys = past + t_new
    k_pos = jnp.arange(n_keys, dtype=jnp.int32)
    q_pos = past + jnp.arange(t_new, dtype=jnp.int32)
    b = rel_bias[_rel_bucket(q_pos[None, :] - k_pos[:, None])]
    b = b - rel_bias[REL_BUCKETS - 1][None, None, :]
    b = jnp.where((q_pos[None, :] >= k_pos[:, None])[:, :, None], b, MASKED)
    b = jnp.transpose(b.reshape(n_keys, t_new, DA_KV_HEADS, DA_GROUP), (0, 2, 3, 1))
    b = jnp.broadcast_to(b[:, :, None], (n_keys, DA_KV_HEADS, 2, DA_GROUP, t_new)).reshape(n_keys, 2 * DA_HEADS * t_new)
    newk = jnp.arange(t_new)
    mm = jnp.where(newk[None, :] >= newk[:, None], 0.0, MASKED).astype(F32)
    mm = jnp.tile(mm, (1, MLA_HEADS))
    cos, sin = _rope_angles(k_pos)
    ctab = jnp.concatenate([cos, cos], axis=1) * gk_rope[None, :]
    stab = jnp.concatenate([sin, sin], axis=1) * gk_rope[None, :]
    return b.astype(F32), mm, ctab.T, stab.T


def _sample_query_layouts(qd, qm, n_seq, t_new):
    q = qd.reshape(n_seq, t_new, DA_KV_HEADS, DA_GROUP, 2, DA_HEAD_DIM)
    q = jnp.transpose(q, (0, 2, 4, 5, 3, 1)).reshape(n_seq, DA_KV_HEADS * 2, DA_HEAD_DIM, DA_GROUP * t_new)
    eye = jnp.eye(DA_KV_HEADS * 2, dtype=qd.dtype)
    qbd = q[:, :, :, None, :] * eye[None, :, None, :, None]
    qbd = qbd.reshape(n_seq, DA_KV_HEADS * 2 * DA_HEAD_DIM, DA_KV_HEADS * 2 * DA_GROUP * t_new)
    qh = qm.reshape(n_seq, t_new, MLA_HEADS, LANES)
    qn = jnp.transpose(qh[..., :MLA_NOPE], (0, 2, 3, 1))
    eye_h = jnp.eye(MLA_HEADS, dtype=qm.dtype)
    qmn = (qn[:, :, :, None, :] * eye_h[None, :, None, :, None]).reshape(n_seq, MLA_HEADS * MLA_NOPE, MLA_HEADS * t_new)
    qr = jnp.transpose(qh[..., ROPE_LO:ROPE_LO + MLA_ROPE], (0, 3, 2, 1)).reshape(n_seq, MLA_ROPE, MLA_HEADS * t_new)
    qrs = jnp.concatenate([qr[:, HALF:], -qr[:, :HALF]], axis=1)
    return qbd, qmn, jnp.concatenate([qr, qrs], axis=1)


def kernel(x_prompt, x_sample, cache_diff_k, cache_diff_v, cache_mla_latent, cache_mla_krope, page_table, rel_bias,
           norm_g, w_in, da_q_norm, da_k_norm, lam_q1, lam_k1, lam_q2, lam_k2, da_out_norm, w_pa, mla_cq_norm, w_uq,
           mla_ckv_norm, w_ukv, mla_q_norm, mla_k_norm, w_pb, w_o):
    b, s, _ = x_prompt.shape
    n_seq, t_new, _ = x_sample.shape
    depth, pool, page = cache_diff_k.shape[:3]
    past = page_table.shape[1] * page
    page_table = page_table.astype(jnp.int32)

    cos_p, sin_p = _rope_tables128(jnp.arange(s, dtype=jnp.int32))
    rows_s = min(TOKEN_TILE, n_seq * t_new)
    cos_s, sin_s = _rope_tables128(past + (jnp.arange(rows_s, dtype=jnp.int32) % t_new))
    bias_diag, bias_sub = _prompt_bias_tiles(rel_bias)

    xp = x_prompt.reshape(b * s, D_MODEL)
    xs = x_sample.reshape(n_seq * t_new, D_MODEL)
    news = [[] for _ in range(8)]
    for l in range(depth):
        p = dict(norm_g=norm_g[l], w_in=w_in[l], da_q_norm=da_q_norm[l], da_k_norm=da_k_norm[l],
                 da_out_norm=da_out_norm[l], w_pa=w_pa[l], mla_cq_norm=mla_cq_norm[l], w_uq=w_uq[l],
                 mla_ckv_norm=mla_ckv_norm[l], w_ukv=w_ukv[l], mla_q_norm=mla_q_norm[l],
                 mla_k_norm=mla_k_norm[l], w_pb=w_pb[l], w_o=w_o[l])
        w = _prep_layer(p, t_new)
        lam_init = 0.8 - 0.6 * math.exp(-0.3 * l)
        lam = (jnp.exp(jnp.sum(lam_q1[l] * lam_k1[l]).astype(F32))
               - jnp.exp(jnp.sum(lam_q2[l] * lam_k2[l]).astype(F32)) + lam_init).reshape(1)

        qd, kd, kdb, vd, vdb, c, kr, qm, km, vm = _project(xp, w, cos_p, sin_p, True)
        as3 = lambda a: a.reshape(b, s, a.shape[-1])
        oa_p = _diff_prompt(lam, as3(qd), as3(kdb), as3(vdb), bias_diag, bias_sub)
        ob_p = _mla_prompt(as3(qm), as3(km), as3(vm))

        qd_s, kd_s, _, vd_s, _, c_s, kr_s, qm_s = _project(xs, w, cos_s, sin_s, False)
        qbd, qmn, qrr = _sample_query_layouts(qd_s, qm_s, n_seq, t_new)
        bias_tab, mla_mask, ctab, stab = _sample_tables(rel_bias, w['gk_rope'], past, t_new)
        new = dict(kd=kd_s.reshape(n_seq, t_new, DA_KV_HEADS * LANES), vd=vd_s.reshape(n_seq, t_new, DA_KV_HEADS * DA_V_DIM),
                   c=c_s.reshape(n_seq, t_new, MLA_KV_RANK),
                   krt=jnp.swapaxes(kr_s.reshape(n_seq, t_new, MLA_ROPE), 1, 2))
        oa_s, ob_s = _sample_attention(
            lam, page_table, cache_diff_k[l].reshape(pool, page * DA_KV_HEADS, LANES),
            cache_diff_v[l].reshape(pool, page * DA_KV_HEADS, DA_V_DIM), cache_mla_latent[l],
            jnp.swapaxes(cache_mla_krope[l], 1, 2), new, qbd, qmn, qrr, w, bias_tab, mla_mask, ctab, stab)

        xp = _merge(xp, oa_p.reshape(b * s, DA_WIDTH), ob_p.reshape(b * s, MLA_WIDTH), w, lam_init)
        xs = _merge(xs, oa_s.reshape(n_seq * t_new, DA_WIDTH), ob_s.reshape(n_seq * t_new, MLA_WIDTH), w, lam_init)

        for lst, a in zip(news, (kd.reshape(b, s, DA_KV_HEADS, LANES), vd.reshape(b, s, DA_KV_HEADS, DA_V_DIM),
                                 c.reshape(b, s, MLA_KV_RANK), kr.reshape(b, s, MLA_ROPE),
                                 kd_s.reshape(n_seq, t_new, DA_KV_HEADS, LANES),
                                 vd_s.reshape(n_seq, t_new, DA_KV_HEADS, DA_V_DIM),
                                 c_s.reshape(n_seq, t_new, MLA_KV_RANK), kr_s.reshape(n_seq, t_new, MLA_ROPE))):
            lst.append(a)
    return (xp.reshape(b, s, D_MODEL), xs.reshape(n_seq, t_new, D_MODEL)) + tuple(jnp.stack(a) for a in news)
```

```python
import functools
import math

import numpy as np
import jax
import jax.numpy as jnp
from jax import lax
from jax.experimental import pallas as pl
from jax.experimental.pallas import tpu as pltpu

F32 = jnp.float32
BF16 = jnp.bfloat16

D_MODEL = 1024
DA_HEADS = 8
DA_KV_HEADS = 4
DA_GROUP = DA_HEADS // DA_KV_HEADS
DA_HEAD_DIM = 64
DA_V_DIM = 2 * DA_HEAD_DIM
DA_WIDTH = DA_HEADS * DA_V_DIM
MLA_HEADS = 8
MLA_Q_RANK = 384
MLA_KV_RANK = 256
MLA_NOPE = 64
MLA_ROPE = 32
MLA_QK_DIM = MLA_NOPE + MLA_ROPE
MLA_V_DIM = 64
MLA_WIDTH = MLA_HEADS * MLA_V_DIM
ROPE_THETA = 10000.0
REL_BUCKETS = 32
REL_MAX_DIST = 128
EPS = 1e-6
IN_SPLITS = (DA_HEADS * 2 * DA_HEAD_DIM, DA_KV_HEADS * 2 * DA_HEAD_DIM, DA_KV_HEADS * DA_V_DIM, DA_WIDTH,
             MLA_Q_RANK, MLA_KV_RANK, MLA_ROPE, MLA_WIDTH, D_MODEL, D_MODEL)

LANES = 128
VMEM_LIMIT_BYTES = 56 * 1024 * 1024

MASKED = -1e30
HALF = MLA_ROPE // 2
ROPE_LO = MLA_NOPE
NOPE_HALF = MLA_NOPE // 2
SUM_ROWS = 16
LOG2E = math.log2(math.e)

_Q0, _K0, _V0, _CQ0, _CKV0, _KR0, _MAIN_COLS = 0, 1024, 1536, 2048, 2432, 2688, 2816

TOKEN_TILE = 512
ATT_TILE = 512
Q_CHUNK = 256
PAGES_PER_STEP = 8
PAGES_PER_BLOCK = 2


def _const_spec(shape):
    nd = len(shape)
    return pl.BlockSpec(shape, lambda *_: (0,) * nd, pipeline_mode=pl.Buffered(1))


def _dot(a, b):
    return jnp.dot(a, b, preferred_element_type=F32)


def _dot_tn(a, b):
    return lax.dot_general(a, b, (((0,), (0,)), ((), ())), preferred_element_type=F32)


def _lane_iota():
    return lax.broadcasted_iota(jnp.int32, (1, LANES), 1)


def _rope128(y, cos_t, sin_t):
    lane = _lane_iota()
    swapped = jnp.where(lane < ROPE_LO + HALF, pltpu.roll(y, LANES - HALF, 1), pltpu.roll(y, HALF, 1))
    return y * cos_t + swapped * sin_t


def _project_body(prompt, x_ref, ng_ref, wm_ref, gq_ref, gk_ref, gcq_ref, wuq_ref, gmq_ref, gckv_ref,
                  wuk_ref, wuv_ref, gmk_ref, cos_ref, sin_ref, *out_refs):
    if prompt:
        qd_ref, kd_ref, kdb_ref, vd_ref, vdt_ref, c_ref, kr_ref, qm_ref, km_ref, vmt_ref = out_refs
    else:
        qd_ref, kd_ref, vd_ref, c_ref, kr_ref, qm_ref = out_refs
    x = x_ref[...]
    h = x * lax.rsqrt(jnp.mean(x * x, axis=-1, keepdims=True) + EPS) * ng_ref[...]
    hb = h.astype(BF16)
    lane = _lane_iota()
    lo = lane < DA_HEAD_DIM
    cos_t = cos_ref[...]
    sin_t = sin_ref[...]

    def pair_norm(blk, g):
        sq = blk * blk
        s_lo = jnp.sum(jnp.where(lo, sq, 0.0), axis=-1, keepdims=True)
        s_hi = jnp.sum(jnp.where(lo, 0.0, sq), axis=-1, keepdims=True)
        r = jnp.where(lo, lax.rsqrt(s_lo / DA_HEAD_DIM + EPS), lax.rsqrt(s_hi / DA_HEAD_DIM + EPS))
        return blk * r * g

    def head_norm(blk, g):
        ss = jnp.sum(blk * blk, axis=-1, keepdims=True)
        return blk * lax.rsqrt(ss / MLA_QK_DIM + EPS) * g

    def put_q(ref, hq, val):
        sl = slice(hq * LANES, (hq + 1) * LANES)
        if prompt:
            ref[sl, :] = jnp.transpose(val).astype(BF16)
        else:
            ref[:, sl] = val.astype(BF16)

    qa = _dot(hb, wm_ref[:, _Q0:_K0])
    gq = gq_ref[...]
    for hq in range(DA_HEADS):
        put_q(qd_ref, hq, pair_norm(qa[:, hq * LANES:(hq + 1) * LANES], gq))
    ka = _dot(hb, wm_ref[:, _K0:_V0])
    gk = gk_ref[...]
    for hk in range(DA_KV_HEADS):
        sl = slice(hk * LANES, (hk + 1) * LANES)
        kn = pair_norm(ka[:, sl], gk)
        kd_ref[:, sl] = kn
        if prompt:
            kdb_ref[:, sl] = kn.astype(BF16)
    va = _dot(hb, wm_ref[:, _V0:_CQ0])
    vd_ref[...] = va
    if prompt:
        ones_rows = jnp.where(lax.broadcasted_iota(jnp.int32, (SUM_ROWS, x.shape[0]), 0) == 0, 1.0, 0.0).astype(BF16)
        for hk in range(DA_KV_HEADS):
            r0 = hk * (DA_V_DIM + SUM_ROWS)
            vdt_ref[r0:r0 + DA_V_DIM, :] = jnp.transpose(va[:, hk * LANES:(hk + 1) * LANES]).astype(BF16)
            vdt_ref[r0 + DA_V_DIM:r0 + DA_V_DIM + SUM_ROWS, :] = ones_rows

    cq = _dot(hb, wm_ref[:, _CQ0:_CKV0])
    cq = cq * lax.rsqrt(jnp.mean(cq * cq, axis=-1, keepdims=True) + EPS) * gcq_ref[...]
    qm = _dot(cq.astype(BF16), wuq_ref[...])
    gmq = gmq_ref[...]
    for hm in range(MLA_HEADS):
        put_q(qm_ref, hm, _rope128(head_norm(qm[:, hm * LANES:(hm + 1) * LANES], gmq), cos_t, sin_t))

    ckv = _dot(hb, wm_ref[:, _CKV0:_KR0])
    c = ckv * lax.rsqrt(jnp.mean(ckv * ckv, axis=-1, keepdims=True) + EPS) * gckv_ref[...]
    c_ref[...] = c
    krp = _dot(hb, wm_ref[:, _KR0:_MAIN_COLS])
    kr_ref[...] = krp[:, ROPE_LO:ROPE_LO + MLA_ROPE]

    if prompt:
        cb = c.astype(BF16)
        kn = _dot(cb, wuk_ref[...])
        gmk = gmk_ref[...]
        for hm in range(MLA_HEADS):
            sl = slice(hm * LANES, (hm + 1) * LANES)
            km_ref[:, sl] = _rope128(head_norm(kn[:, sl] + krp, gmk), cos_t, sin_t).astype(BF16)
        vm = _dot(cb, wuv_ref[...])
        for blk in range(MLA_WIDTH // LANES):
            pair = jnp.transpose(vm[:, blk * LANES:(blk + 1) * LANES]).astype(BF16)
            for hh in range(2):
                r0 = (2 * blk + hh) * (MLA_V_DIM + SUM_ROWS)
                vmt_ref[r0:r0 + MLA_V_DIM, :] = pair[hh * MLA_V_DIM:(hh + 1) * MLA_V_DIM, :]
                vmt_ref[r0 + MLA_V_DIM:r0 + MLA_V_DIM + SUM_ROWS, :] = ones_rows


def _project(x2d, w, cos_tab, sin_tab, prompt, batch=None):
    n = x2d.shape[0]
    tm = min(TOKEN_TILE, n)
    assert n % tm == 0 and cos_tab.shape[0] % tm == 0
    n_pos = cos_tab.shape[0] // tm
    row = lambda cols: pl.BlockSpec((tm, cols), lambda i: (i, 0))
    pos = pl.BlockSpec((tm, LANES), lambda i: (i % n_pos, 0))
    consts = [w['norm_g'], w['w_main'], w['gq'], w['gk'], w['gcq'], w['w_uq'], w['gmq'], w['gckv'],
              w['w_uk_pad'], w['w_uv'], w['gmk']]
    sds = jax.ShapeDtypeStruct
    if prompt:
        s = cos_tab.shape[0]
        assert batch * s == n
        tr = lambda rows: pl.BlockSpec((None, rows, tm), lambda i: (i // n_pos, 0, i % n_pos))
        vdt_rows = DA_KV_HEADS * (DA_V_DIM + SUM_ROWS)
        vmt_rows = MLA_HEADS * (MLA_V_DIM + SUM_ROWS)
        out_shape = [sds((batch, DA_WIDTH, s), BF16), sds((n, 512), F32), sds((n, 512), BF16), sds((n, 512), F32),
                     sds((batch, vdt_rows, s), BF16), sds((n, MLA_KV_RANK), F32), sds((n, MLA_ROPE), F32),
                     sds((batch, MLA_HEADS * LANES, s), BF16), sds((n, MLA_HEADS * LANES), BF16),
                     sds((batch, vmt_rows, s), BF16)]
        out_specs = [tr(DA_WIDTH), row(512), row(512), row(512), tr(vdt_rows), row(MLA_KV_RANK), row(MLA_ROPE),
                     tr(MLA_HEADS * LANES), row(MLA_HEADS * LANES), tr(vmt_rows)]
    else:
        out_shape = [sds((n, DA_WIDTH), BF16), sds((n, 512), F32), sds((n, 512), F32), sds((n, MLA_KV_RANK), F32),
                     sds((n, MLA_ROPE), F32), sds((n, MLA_HEADS * LANES), BF16)]
        out_specs = [row(DA_WIDTH), row(512), row(512), row(MLA_KV_RANK), row(MLA_ROPE), row(MLA_HEADS * LANES)]
    return pl.pallas_call(
        functools.partial(_project_body, prompt),
        grid=(n // tm,),
        in_specs=[row(D_MODEL)] + [_const_spec(a.shape) for a in consts] + [pos, pos],
        out_specs=out_specs,
        out_shape=out_shape,
        compiler_params=pltpu.CompilerParams(dimension_semantics=("parallel",), vmem_limit_bytes=VMEM_LIMIT_BYTES),
        name="project_prompt" if prompt else "project_sample",
    )(x2d, *consts, cos_tab, sin_tab)


def _absorb_chunks(chunks, m_ref, acc_ref):
    stats = []
    for s, _, cols in chunks:
        m_old = m_ref[:, cols]
        m_new = jnp.maximum(m_old, jnp.max(s, axis=0, keepdims=True))
        m_ref[:, cols] = m_new
        stats.append((m_new, jnp.exp2(m_old - m_new)))
    probs = [jnp.exp2(s - m_new).astype(BF16) for (s, _, _), (m_new, _) in zip(chunks, stats)]
    for (_, vt, cols), (_, alpha), p in zip(chunks, stats, probs):
        acc_ref[:, cols] = alpha * acc_ref[:, cols] + _dot(vt, p)


def _key_tile_pipeline(i, qk, absorb):
    qk(0, 0)

    @pl.when(i == 0)
    def _():
        absorb(0, 0)

    @pl.when(i >= 1)
    def _():
        qk(1, 1)
        absorb(0, 0)

        @pl.when(i == 1)
        def _():
            absorb(1, 1)

        @pl.when(i >= 2)
        def _():
            qk(2, 0)
            absorb(1, 1)
            n_pairs = (i - 2) // 2

            def pair(p, carry):
                r = 2 + 2 * p
                qk(r + 1, 1)
                absorb(r, 0)
                qk(r + 2, 0)
                absorb(r + 1, 1)
                return carry
            lax.fori_loop(0, n_pairs, pair, 0)
            r = 2 + 2 * n_pairs

            @pl.when(r == i)
            def _():
                absorb(r, 0)

            @pl.when(r != i)
            def _():
                qk(r + 1, 1)
                absorb(r, 0)
                absorb(r + 1, 1)


def _diff_prompt_body(lam_ref, qt_ref, k_ref, vt_ref, bdiag_ref, bsub_ref, o_ref,
                      qs_ref, s0_ref, s1_ref, m_ref, acc_ref, o1_ref):
    i = pl.program_id(2)
    mp = pl.program_id(3)
    t = ATT_TILE
    ch = Q_CHUNK
    per_g = t // ch
    n_chunks = DA_GROUP * per_g
    s_bufs = (s0_ref, s1_ref)

    row = lax.broadcasted_iota(jnp.int32, (LANES, 1), 0)
    keep = (row < DA_HEAD_DIM) == (mp == 0)
    for g in range(DA_GROUP):
        qg = qt_ref[g * LANES:(g + 1) * LANES, :]
        qs_ref[:, g * t:(g + 1) * t] = jnp.where(keep, qg, jnp.zeros_like(qg))
    m_ref[...] = jnp.full(m_ref.shape, MASKED, F32)
    acc_ref[...] = jnp.zeros(acc_ref.shape, F32)

    def n_keys(r, cc):
        return (cc + 1) * ch if isinstance(r, int) and r == 0 else t

    def qk(r, buf):
        ks = pl.multiple_of((i - r) * t, t)
        for c in range(n_chunks):
            nk = n_keys(r, c % per_g)
            cols = slice(c * ch, (c + 1) * ch)
            s_bufs[buf][0:nk, cols] = _dot(k_ref[pl.ds(ks, nk), :], qs_ref[:, cols])

    def absorb(r, buf):
        ks = pl.multiple_of((i - r) * t, t)
        chunks = []
        for c in range(n_chunks):
            g, cc = divmod(c, per_g)
            nk = n_keys(r, cc)
            cols = slice(c * ch, (c + 1) * ch)
            s = s_bufs[buf][0:nk, cols]
            if isinstance(r, int) and r == 0:
                s = s + bdiag_ref[g, 0:nk, cc * ch:(cc + 1) * ch]
            elif isinstance(r, int) and r == 1 and cc == 0:
                s = s + bsub_ref[g]
            chunks.append((s, vt_ref[:, pl.ds(ks, nk)], cols))
        _absorb_chunks(chunks, m_ref, acc_ref)

    _key_tile_pipeline(i, qk, absorb)

    o = acc_ref[0:DA_V_DIM, :] * (1.0 / acc_ref[DA_V_DIM:DA_V_DIM + 1, :])

    @pl.when(mp == 0)
    def _():
        o1_ref[...] = o

    @pl.when(mp == 1)
    def _():
        comb = o1_ref[...] - lam_ref[0] * o
        for g in range(DA_GROUP):
            o_ref[:, g * LANES:(g + 1) * LANES] = jnp.transpose(comb[:, g * t:(g + 1) * t])


def _diff_prompt(lam, qdt, kdb, vdt, bias_diag, bias_sub):
    b, _, s = qdt.shape
    t = ATT_TILE
    assert s % t == 0
    vrows = DA_V_DIM + SUM_ROWS
    return pl.pallas_call(
        _diff_prompt_body,
        grid=(b, DA_KV_HEADS, s // t, 2),
        in_specs=[pl.BlockSpec(memory_space=pltpu.SMEM),
                  pl.BlockSpec((None, DA_GROUP * LANES, t), lambda bi, h, i, mp: (bi, h, i)),
                  pl.BlockSpec((None, s, LANES), lambda bi, h, i, mp: (bi, 0, h)),
                  pl.BlockSpec((None, vrows, s), lambda bi, h, i, mp: (bi, h, 0)),
                  pl.BlockSpec((DA_GROUP, t, t), lambda bi, h, i, mp: (h, 0, 0)),
                  pl.BlockSpec((DA_GROUP, t, Q_CHUNK), lambda bi, h, i, mp: (h, 0, 0))],
        out_specs=pl.BlockSpec((None, t, DA_GROUP * LANES), lambda bi, h, i, mp: (bi, i, h)),
        out_shape=jax.ShapeDtypeStruct((b, s, DA_WIDTH), F32),
        scratch_shapes=[pltpu.VMEM((LANES, DA_GROUP * t), BF16),
                        pltpu.VMEM((t, DA_GROUP * t), F32),
                        pltpu.VMEM((t, DA_GROUP * t), F32),
                        pltpu.VMEM((1, DA_GROUP * t), F32),
                        pltpu.VMEM((vrows, DA_GROUP * t), F32),
                        pltpu.VMEM((DA_V_DIM, DA_GROUP * t), F32)],
        compiler_params=pltpu.CompilerParams(
            dimension_semantics=("parallel", "parallel", "arbitrary", "arbitrary"), vmem_limit_bytes=VMEM_LIMIT_BYTES),
        name="diff_prompt",
    )(lam, qdt, kdb, vdt, bias_diag, bias_sub)


def _mla_prompt_body(qt_ref, k_ref, vt_ref, o_ref, s0_ref, s1_ref, m_ref, acc_ref, pair_ref):
    i = pl.program_id(2)
    hh = pl.program_id(3)
    t = ATT_TILE
    ch = Q_CHUNK
    n_chunks = t // ch
    scale2 = (MLA_QK_DIM ** -0.5) * LOG2E
    s_bufs = (s0_ref, s1_ref)
    m_ref[...] = jnp.full(m_ref.shape, MASKED, F32)
    acc_ref[...] = jnp.zeros(acc_ref.shape, F32)

    def n_keys(r, c):
        return (c + 1) * ch if isinstance(r, int) and r == 0 else t

    def qk(r, buf):
        ks = pl.multiple_of((i - r) * t, t)
        for c in range(n_chunks):
            nk = n_keys(r, c)
            cols = slice(c * ch, (c + 1) * ch)
            s_bufs[buf][0:nk, cols] = _dot(k_ref[pl.ds(ks, nk), :], qt_ref[:, cols])

    def absorb(r, buf):
        ks = pl.multiple_of((i - r) * t, t)
        chunks = []
        for c in range(n_chunks):
            nk = n_keys(r, c)
            cols = slice(c * ch, (c + 1) * ch)
            s = s_bufs[buf][0:nk, cols] * scale2
            if isinstance(r, int) and r == 0:
                kidx = lax.broadcasted_iota(jnp.int32, (nk, ch), 0)
                qidx = lax.broadcasted_iota(jnp.int32, (nk, ch), 1) + c * ch
                s = jnp.where(kidx <= qidx, s, MASKED)
            chunks.append((s, vt_ref[:, pl.ds(ks, nk)], cols))
        _absorb_chunks(chunks, m_ref, acc_ref)

    _key_tile_pipeline(i, qk, absorb)

    o = acc_ref[0:MLA_V_DIM, :] * (1.0 / acc_ref[MLA_V_DIM:MLA_V_DIM + 1, :])
    pair_ref[pl.ds(pl.multiple_of(hh * MLA_V_DIM, MLA_V_DIM), MLA_V_DIM), :] = o

    @pl.when(hh == 1)
    def _():
        o_ref[...] = jnp.transpose(pair_ref[...])


def _mla_prompt(qmt, km, vmt):
    b, _, s = qmt.shape
    t = ATT_TILE
    vrows = MLA_V_DIM + SUM_ROWS
    return pl.pallas_call(
        _mla_prompt_body,
        grid=(b, MLA_HEADS // 2, s // t, 2),
        in_specs=[pl.BlockSpec((None, LANES, t), lambda bi, hp, i, hh: (bi, 2 * hp + hh, i)),
                  pl.BlockSpec((None, s, LANES), lambda bi, hp, i, hh: (bi, 0, 2 * hp + hh)),
                  pl.BlockSpec((None, vrows, s), lambda bi, hp, i, hh: (bi, 2 * hp + hh, 0))],
        out_specs=pl.BlockSpec((None, t, 2 * MLA_V_DIM), lambda bi, hp, i, hh: (bi, i, hp)),
        out_shape=jax.ShapeDtypeStruct((b, s, MLA_WIDTH), F32),
        scratch_shapes=[pltpu.VMEM((t, t), F32), pltpu.VMEM((t, t), F32),
                        pltpu.VMEM((1, t), F32),
                        pltpu.VMEM((vrows, t), F32),
                        pltpu.VMEM((2 * MLA_V_DIM, t), F32)],
        compiler_params=pltpu.CompilerParams(
            dimension_semantics=("parallel", "parallel", "arbitrary", "arbitrary"), vmem_limit_bytes=VMEM_LIMIT_BYTES),
        name="mla_prompt",
    )(qmt, km, vmt)


def _sample_body(n_steps, pt_ref, lam_ref, *refs):
    pg = PAGES_PER_STEP
    kpages = refs[0:pg]
    vpages = refs[pg:2 * pg]
    cpages = refs[2 * pg:3 * pg]
    rpages = refs[3 * pg:4 * pg]
    (kn_ref, vn_ref, cn_ref, rn_ref, qbd_ref, qmn_ref, qrr_ref, gkn_ref, wuk_ref, wuv_ref, eseg_ref,
     bias_ref, mmask_ref, ctab_ref, stab_ref,
     oa_ref, ob_ref,
     qabs_ref, md_ref, ld_ref, accd_ref, mm_ref, lm_ref, accm_ref) = refs[4 * pg:]
    j = pl.program_id(1)
    scale_m = (MLA_QK_DIM ** -0.5) * LOG2E
    page = cpages[0].shape[0]
    t_new = kn_ref.shape[0]
    n_mla = MLA_HEADS * t_new

    def heads_to_lanes(ref2):
        t = ref2.shape[0] // DA_KV_HEADS
        return jnp.concatenate([ref2[pl.ds(h, t, stride=DA_KV_HEADS), :].astype(BF16) for h in range(DA_KV_HEADS)],
                               axis=1)

    def rows_from_lanes(alpha, n_rows):
        full = jnp.broadcast_to(alpha, (LANES, alpha.shape[1]))
        if alpha.shape[1] < LANES:
            full = jnp.concatenate([full, jnp.zeros((LANES, LANES - alpha.shape[1]), F32)], axis=1)
        return jnp.transpose(full)[:n_rows, :]

    def score_matmuls(kb, c, krt, ctab, stab):
        cb = c.astype(BF16)
        s = _dot(kb, qbd_ref[...])
        kn = _dot(cb, wuk_ref[...])
        raw = _dot(cb, qabs_ref[...])
        ext = _dot_tn(jnp.concatenate([krt * ctab, krt * stab, krt * krt], axis=0).astype(BF16), qrr_ref[...])
        return cb, s, kn, raw, ext

    def local_softmaxes(blocks):
        half = MLA_HEADS * NOPE_HALF
        ss = []
        for (cb, s, kn, raw, ext), _, _, _ in blocks:
            kn2 = kn * kn
            ss.append(_dot((kn2[:, :half] + kn2[:, half:]).astype(BF16), eseg_ref[...]))
        sa = [s + bias for (_, s, _, _, _), _, bias, _ in blocks]
        ma = [jnp.max(s, axis=0, keepdims=True) for s in sa]
        pa = [jnp.exp2(s - m) for s, m in zip(sa, ma)]
        sb = []
        for ((_, _, _, raw, ext), _, _, mmask), q in zip(blocks, ss):
            r = lax.rsqrt((q + pltpu.roll(ext, LANES - n_mla, 1)[:, :n_mla]) / MLA_QK_DIM + EPS)
            sm = (raw + ext[:, :n_mla]) * r * scale_m
            sb.append(sm if mmask is None else sm + mmask)
        mb = [jnp.max(s, axis=0, keepdims=True) for s in sb]
        pb_ = [jnp.exp2(s - m) for s, m in zip(sb, mb)]
        parts_a = [(m, jnp.sum(p, axis=0, keepdims=True), _dot_tn(p.astype(BF16), vb))
                   for m, p, (_, vb, _, _) in zip(ma, pa, blocks)]
        parts_b = [(m, jnp.sum(p, axis=0, keepdims=True), _dot_tn(p.astype(BF16), pre[0]))
                   for m, p, (pre, _, _, _) in zip(mb, pb_, blocks)]
        return parts_a, parts_b

    def fold(m_ref, l_ref, acc_ref, parts):
        n_rows, width = acc_ref.shape
        m_old = m_ref[...]
        m_new = m_old
        for m, _, _ in parts:
            m_new = jnp.maximum(m_new, m)
        wide = lambda f: jnp.concatenate([rows_from_lanes(f, n_rows)] * (width // LANES), axis=1)
        alpha = jnp.exp2(m_old - m_new)
        l_new = alpha * l_ref[...]
        acc = wide(alpha) * acc_ref[...]
        for m, l, a in parts:
            f = jnp.exp2(m - m_new)
            l_new = l_new + f * l
            acc = acc + wide(f) * a
        m_ref[...] = m_new
        l_ref[...] = l_new
        acc_ref[...] = acc

    @pl.when(j == 0)
    def _():
        qg = (qmn_ref[...].astype(F32) * gkn_ref[...]).astype(BF16)
        qabs_ref[...] = _dot(wuk_ref[...], qg).astype(BF16)
        past = n_steps * pg * page
        rows_n = pl.ds(past, t_new)
        pre = score_matmuls(kn_ref[...].astype(BF16), cn_ref[...], rn_ref[...], ctab_ref[:, rows_n], stab_ref[:, rows_n])
        ((m_d, l_d, pv),), ((m_m, l_m, pc),) = local_softmaxes(
            [(pre, vn_ref[...].astype(BF16), bias_ref[rows_n, :], mmask_ref[...])])
        md_ref[...] = m_d
        ld_ref[...] = l_d
        accd_ref[...] = pv
        mm_ref[...] = m_m
        lm_ref[...] = l_m
        accm_ref[...] = pc

    base = j * (pg * page)
    pb = PAGES_PER_BLOCK
    blocks = []
    for blk in range(pg // pb):
        sel = slice(blk * pb, (blk + 1) * pb)
        rows = pl.ds(pl.multiple_of(base + blk * pb * page, pb * page), pb * page)
        pre = score_matmuls(
            jnp.concatenate([heads_to_lanes(r) for r in kpages[sel]], axis=0),
            jnp.concatenate([r[...] for r in cpages[sel]], axis=0),
            jnp.concatenate([r[...] for r in rpages[sel]], axis=1),
            ctab_ref[:, rows], stab_ref[:, rows])
        blocks.append((pre, jnp.concatenate([heads_to_lanes(r) for r in vpages[sel]], axis=0), bias_ref[rows, :], None))
    parts_d, parts_m = local_softmaxes(blocks)
    fold(md_ref, ld_ref, accd_ref, parts_d)
    fold(mm_ref, lm_ref, accm_ref, parts_m)

    @pl.when(j == n_steps - 1)
    def _():
        lam = lam_ref[0]
        inv_d = rows_from_lanes(1.0 / ld_ref[...], LANES)
        accd = accd_ref[...]
        lanes_per_kv = 2 * DA_GROUP * t_new
        for h in range(DA_KV_HEADS):
            cols = slice(h * LANES, (h + 1) * LANES)
            for g in range(DA_GROUP):
                r1 = h * lanes_per_kv + g * t_new
                r2 = r1 + DA_GROUP * t_new
                o1 = accd[r1:r1 + t_new, cols] * inv_d[r1:r1 + t_new, :]
                o2 = accd[r2:r2 + t_new, cols] * inv_d[r2:r2 + t_new, :]
                hq = h * DA_GROUP + g
                oa_ref[:, hq * LANES:(hq + 1) * LANES] = o1 - lam * o2
        inv_m = rows_from_lanes(1.0 / lm_ref[...], n_mla)
        cbar = (accm_ref[...] * jnp.concatenate([inv_m] * (MLA_KV_RANK // LANES), axis=1)).astype(BF16)
        full = _dot(cbar, wuv_ref[...])
        lane5 = lax.broadcasted_iota(jnp.int32, (1, MLA_WIDTH), 1)
        ob = jnp.zeros((t_new, MLA_WIDTH), F32)
        for h in range(MLA_HEADS):
            sel = (lane5 >= h * MLA_V_DIM) & (lane5 < (h + 1) * MLA_V_DIM)
            ob = ob + jnp.where(sel, full[h * t_new:(h + 1) * t_new, :], 0.0)
        ob_ref[...] = ob


def _sample_attention(lam, page_table, ck, cv, cc, crt, new, qbd, qmn, qrr, w, bias_tab, mla_mask, ctab, stab):
    n_seq, n_pages = page_table.shape
    pool, page = cc.shape[0], cc.shape[1]
    t_new = new['kd'].shape[1]
    pg = PAGES_PER_STEP
    assert n_pages % pg == 0 and pg % PAGES_PER_BLOCK == 0 and MLA_HEADS * t_new <= LANES // 2
    n_steps = n_pages // pg

    def page_spec(*tail):
        zeros = (0,) * len(tail)
        return lambda p: pl.BlockSpec((None,) + tail, lambda b, j, pt: (pt[b, j * pg + p],) + zeros)

    def seq_spec(*tail):
        zeros = (0,) * len(tail)
        return pl.BlockSpec((None,) + tail, lambda b, j, pt: (b,) + zeros)

    def const(shape):
        nd = len(shape)
        return pl.BlockSpec(shape, lambda b, j, pt: (0,) * nd, pipeline_mode=pl.Buffered(1))

    in_specs = [pl.BlockSpec(memory_space=pltpu.SMEM)]
    for mk in (page_spec(page * DA_KV_HEADS, LANES), page_spec(page * DA_KV_HEADS, DA_V_DIM),
               page_spec(page, MLA_KV_RANK), page_spec(MLA_ROPE, page)):
        in_specs += [mk(p) for p in range(pg)]
    per_seq = [new['kd'], new['vd'], new['c'], new['krt'], qbd, qmn, qrr]
    in_specs += [seq_spec(*a.shape[1:]) for a in per_seq]
    consts = [w['gk_nope_col'], w['w_uk'], w['w_uv'], w['eseg'], bias_tab, mla_mask, ctab, stab]
    in_specs += [const(a.shape) for a in consts]
    operands = [ck] * pg + [cv] * pg + [cc] * pg + [crt] * pg + per_seq + consts
    grid_spec = pltpu.PrefetchScalarGridSpec(
        num_scalar_prefetch=1,
        grid=(n_seq, n_steps),
        in_specs=in_specs,
        out_specs=[pl.BlockSpec((None, t_new, DA_WIDTH), lambda b, j, pt: (b, 0, 0)),
                   pl.BlockSpec((None, t_new, MLA_WIDTH), lambda b, j, pt: (b, 0, 0))],
        scratch_shapes=[pltpu.VMEM((MLA_KV_RANK, MLA_HEADS * t_new), BF16),
                        pltpu.VMEM((1, LANES), F32), pltpu.VMEM((1, LANES), F32),
                        pltpu.VMEM((LANES, DA_KV_HEADS * DA_V_DIM), F32),
                        pltpu.VMEM((1, MLA_HEADS * t_new), F32), pltpu.VMEM((1, MLA_HEADS * t_new), F32),
                        pltpu.VMEM((MLA_HEADS * t_new, MLA_KV_RANK), F32)])
    return pl.pallas_call(
        functools.partial(_sample_body, n_steps),
        grid_spec=grid_spec,
        out_shape=[jax.ShapeDtypeStruct((n_seq, t_new, DA_WIDTH), F32),
                   jax.ShapeDtypeStruct((n_seq, t_new, MLA_WIDTH), F32)],
        compiler_params=pltpu.CompilerParams(dimension_semantics=("parallel", "arbitrary"),
                                             vmem_limit_bytes=VMEM_LIMIT_BYTES),
        name="sample_attention",
    )(page_table, lam, *operands)


def _merge_body(one_minus_lam_init, x_ref, oa_ref, ob_ref, ng_ref, wzg_ref, gout_ref, wpa_ref, wpb_ref, wo_ref, y_ref):
    x = x_ref[...]
    h = x * lax.rsqrt(jnp.mean(x * x, axis=-1, keepdims=True) + EPS) * ng_ref[...]
    hb = h.astype(BF16)
    za = _dot(hb, wzg_ref[:, 0:DA_WIDTH])
    gout = gout_ref[...]
    parts = []
    for hq in range(DA_HEADS):
        sl = slice(hq * LANES, (hq + 1) * LANES)
        o = oa_ref[:, sl]
        o = o * lax.rsqrt(jnp.mean(o * o, axis=-1, keepdims=True) + EPS) * gout * one_minus_lam_init
        z = za[:, sl]
        parts.append((o * (z * jax.nn.sigmoid(z))).astype(BF16))
    ya = _dot(jnp.concatenate(parts, axis=1), wpa_ref[...])
    zb = _dot(hb, wzg_ref[:, DA_WIDTH:DA_WIDTH + MLA_WIDTH])
    yb = _dot((ob_ref[...] * (zb * jax.nn.sigmoid(zb))).astype(BF16), wpb_ref[...])
    g0 = DA_WIDTH + MLA_WIDTH
    ga = _dot(hb, wzg_ref[:, g0:g0 + D_MODEL])
    gb = _dot(hb, wzg_ref[:, g0 + D_MODEL:g0 + 2 * D_MODEL])
    mix = jax.nn.sigmoid(ga) * ya + jax.nn.sigmoid(gb) * yb
    y_ref[...] = x + _dot(mix.astype(BF16), wo_ref[...])


def _merge(x2d, oa, ob, w, lam_init):
    n = x2d.shape[0]
    tm = min(TOKEN_TILE, n)
    assert n % tm == 0
    row = lambda cols: pl.BlockSpec((tm, cols), lambda i: (i, 0))
    consts = [w['norm_g'], w['w_zg'], w['gout'], w['w_pa'], w['w_pb'], w['w_o']]
    return pl.pallas_call(
        functools.partial(_merge_body, 1.0 - lam_init),
        grid=(n // tm,),
        in_specs=[row(D_MODEL), row(DA_WIDTH), row(MLA_WIDTH)] + [_const_spec(a.shape) for a in consts],
        out_specs=row(D_MODEL),
        out_shape=jax.ShapeDtypeStruct((n, D_MODEL), F32),
        compiler_params=pltpu.CompilerParams(dimension_semantics=("parallel",), vmem_limit_bytes=VMEM_LIMIT_BYTES),
        name="merge",
    )(x2d, oa, ob, *consts)


def _rel_bucket(dist):
    n = jnp.maximum(dist, 0)
    max_exact = REL_BUCKETS // 2
    nf = jnp.maximum(n, 1).astype(F32)
    large = max_exact + (jnp.log(nf / max_exact) / math.log(REL_MAX_DIST / max_exact)
                         * (REL_BUCKETS - max_exact)).astype(jnp.int32)
    large = jnp.minimum(large, REL_BUCKETS - 1)
    return jnp.where(n < max_exact, n, large)


def _rope_angles(pos):
    inv = jnp.power(ROPE_THETA, -jnp.arange(HALF, dtype=F32) / HALF)
    ang = pos.astype(F32)[:, None] * inv[None, :]
    return jnp.cos(ang), jnp.sin(ang)


def _rope_tables128(pos):
    cos, sin = _rope_angles(pos)
    n = pos.shape[0]
    cos_t = jnp.concatenate([jnp.ones((n, MLA_NOPE), F32), cos, cos, jnp.zeros((n, LANES - MLA_QK_DIM), F32)], axis=1)
    sin_t = jnp.concatenate([jnp.zeros((n, MLA_NOPE), F32), -sin, sin, jnp.zeros((n, LANES - MLA_QK_DIM), F32)], axis=1)
    return cos_t, sin_t


def _pad_heads(a, live):
    pad = [(0, 0)] * (a.ndim - 1) + [(0, LANES - live)]
    a = jnp.pad(a, pad)
    return a.reshape(a.shape[:-2] + (a.shape[-2] * LANES,))


def _prep_layer(p, t_new):
    offs = [int(v) for v in np.cumsum((0,) + IN_SPLITS)]
    wq, wk, wv, wza, wcq, wckv, wkr, wzb, wga, wgb = [p['w_in'][:, offs[n]:offs[n + 1]] for n in range(10)]
    wkr_placed = jnp.pad(wkr, ((0, 0), (ROPE_LO, LANES - ROPE_LO - MLA_ROPE)))
    w_ukv = p['w_ukv'].reshape(MLA_KV_RANK, MLA_HEADS, MLA_NOPE + MLA_V_DIM)
    w_uk = w_ukv[:, :, :MLA_NOPE]
    w_uk_halves = jnp.transpose(w_uk.reshape(MLA_KV_RANK, MLA_HEADS, 2, NOPE_HALF), (0, 2, 1, 3))
    gmk = p['mla_k_norm']
    gk_nope = jnp.broadcast_to(gmk[:MLA_NOPE].reshape(2, 1, NOPE_HALF), (2, MLA_HEADS, NOPE_HALF))
    head_of_row = np.arange(MLA_HEADS * NOPE_HALF) // NOPE_HALF
    head_of_lane = np.arange(MLA_HEADS * t_new) // t_new
    return dict(
        norm_g=p['norm_g'].reshape(1, D_MODEL),
        w_main=jnp.concatenate([wq, wk, wv, wcq, wckv, wkr_placed], axis=1).astype(BF16),
        w_zg=jnp.concatenate([wza, wzb, wga, wgb], axis=1).astype(BF16),
        gq=p['da_q_norm'].reshape(1, LANES) * ((DA_HEAD_DIM ** -0.5) * LOG2E),
        gk=p['da_k_norm'].reshape(1, LANES),
        gcq=p['mla_cq_norm'].reshape(1, MLA_Q_RANK),
        w_uq=_pad_heads(p['w_uq'].reshape(MLA_Q_RANK, MLA_HEADS, MLA_QK_DIM), MLA_QK_DIM).astype(BF16),
        gmq=jnp.pad(p['mla_q_norm'], (0, LANES - MLA_QK_DIM)).reshape(1, LANES),
        gckv=p['mla_ckv_norm'].reshape(1, MLA_KV_RANK),
        w_uk_pad=_pad_heads(w_uk, MLA_NOPE).astype(BF16),
        w_uk=w_uk_halves.reshape(MLA_KV_RANK, MLA_HEADS * MLA_NOPE).astype(BF16),
        w_uv=w_ukv[:, :, MLA_NOPE:].reshape(MLA_KV_RANK, MLA_WIDTH).astype(BF16),
        gmk=jnp.pad(gmk, (0, LANES - MLA_QK_DIM)).reshape(1, LANES),
        gk_nope_col=gk_nope.reshape(MLA_HEADS * MLA_NOPE, 1),
        gk_rope=gmk[MLA_NOPE:],
        eseg=jnp.asarray(head_of_row[:, None] == head_of_lane[None, :], BF16),
        gout=p['da_out_norm'].reshape(1, DA_V_DIM),
        w_pa=p['w_pa'].astype(BF16), w_pb=p['w_pb'].astype(BF16), w_o=p['w_o'].astype(BF16),
    )


def _toeplitz(v, n):
    lead = v.shape[:-1]
    u = jnp.flip(v, axis=-1)
    w = jnp.concatenate([u, jnp.zeros(lead + (1,), v.dtype)], axis=-1)
    flat = jnp.tile(w, (1,) * len(lead) + (n,))[..., :n * (2 * n - 1)]
    return flat.reshape(lead + (n, 2 * n - 1))[..., n - 1:]


def _prompt_bias_tiles(rel_bias):
    t = ATT_TILE
    far = rel_bias[REL_BUCKETS - 1]
    d = jnp.arange(2 * t - 1, dtype=jnp.int32) - (t - 1)
    f = jnp.where((d >= 0)[:, None], rel_bias[_rel_bucket(d)] - far[None, :], MASKED)
    diag = _toeplitz(f.T, t)
    dc = LANES + jnp.arange(2 * LANES - 1, dtype=jnp.int32) - (LANES - 1)
    corner = _toeplitz((rel_bias[_rel_bucket(dc)] - far[None, :]).T, LANES)
    sub = jnp.pad(jnp.swapaxes(corner, 1, 2), ((0, 0), (t - LANES, 0), (0, Q_CHUNK - LANES)))
    return jnp.swapaxes(diag, 1, 2) * LOG2E, sub * LOG2E


def _sample_tables(rel_bias, gk_rope, past, t_new):
    n_keys = past + t_new
    near = REL_MAX_DIST + t_new
    k_pos = jnp.arange(n_keys - near, n_keys, dtype=jnp.int32)
    q_pos = past + jnp.arange(t_new, dtype=jnp.int32)
    b = rel_bias[_rel_bucket(q_pos[None, :] - k_pos[:, None])]
    b = b - rel_bias[REL_BUCKETS - 1][None, None, :]
    b = jnp.where((q_pos[None, :] >= k_pos[:, None])[:, :, None], b, MASKED)
    b = jnp.transpose(b.reshape(near, t_new, DA_KV_HEADS, DA_GROUP), (0, 2, 3, 1))
    b = jnp.broadcast_to(b[:, :, None], (near, DA_KV_HEADS, 2, DA_GROUP, t_new)).reshape(near, 2 * DA_HEADS * t_new)
    b = jnp.pad(b.astype(F32) * LOG2E, ((n_keys - near, 0), (0, 0)))
    newk = jnp.arange(t_new)
    mm = jnp.where(newk[None, :] >= newk[:, None], 0.0, MASKED).astype(F32)
    mm = jnp.tile(mm, (1, MLA_HEADS))
    cos, sin = _rope_angles(jnp.arange(n_keys, dtype=jnp.int32))
    ctab = jnp.concatenate([cos, cos], axis=1) * gk_rope[None, :]
    stab = jnp.concatenate([sin, sin], axis=1) * gk_rope[None, :]
    return b, mm, ctab.T, stab.T


def _sample_query_layouts(qd, qm, n_seq, t_new):
    q = qd.reshape(n_seq, t_new, DA_KV_HEADS, DA_GROUP, 2, DA_HEAD_DIM)
    q = jnp.transpose(q, (0, 2, 4, 5, 3, 1)).reshape(n_seq, DA_KV_HEADS * 2, DA_HEAD_DIM, DA_GROUP * t_new)
    eye = jnp.eye(DA_KV_HEADS * 2, dtype=qd.dtype)
    qbd = q[:, :, :, None, :] * eye[None, :, None, :, None]
    qbd = qbd.reshape(n_seq, DA_KV_HEADS * 2 * DA_HEAD_DIM, DA_KV_HEADS * 2 * DA_GROUP * t_new)
    qh = qm.reshape(n_seq, t_new, MLA_HEADS, LANES)
    qn = jnp.transpose(qh[..., :MLA_NOPE].reshape(n_seq, t_new, MLA_HEADS, 2, NOPE_HALF), (0, 3, 2, 4, 1))
    eye_h = jnp.eye(MLA_HEADS, dtype=qm.dtype)
    qmn = (qn[:, :, :, :, None, :] * eye_h[None, None, :, None, :, None]).reshape(
        n_seq, MLA_HEADS * MLA_NOPE, MLA_HEADS * t_new)
    qr = jnp.transpose(qh[..., ROPE_LO:ROPE_LO + MLA_ROPE], (0, 3, 2, 1)).reshape(n_seq, MLA_ROPE, MLA_HEADS * t_new)
    qrs = jnp.concatenate([qr[:, HALF:], -qr[:, :HALF]], axis=1)
    n_mla = MLA_HEADS * t_new
    top = jnp.pad(jnp.concatenate([qr, qrs], axis=1), ((0, 0), (0, 0), (0, LANES - n_mla)))
    bottom = jnp.pad(jnp.ones((n_seq, MLA_ROPE, n_mla), qm.dtype), ((0, 0), (0, 0), (LANES - n_mla, 0)))
    return qbd, qmn, jnp.concatenate([top, bottom], axis=1)


def kernel(x_prompt, x_sample, cache_diff_k, cache_diff_v, cache_mla_latent, cache_mla_krope, page_table, rel_bias,
           norm_g, w_in, da_q_norm, da_k_norm, lam_q1, lam_k1, lam_q2, lam_k2, da_out_norm, w_pa, mla_cq_norm, w_uq,
           mla_ckv_norm, w_ukv, mla_q_norm, mla_k_norm, w_pb, w_o):
    b, s, _ = x_prompt.shape
    n_seq, t_new, _ = x_sample.shape
    depth, pool, page = cache_diff_k.shape[:3]
    past = page_table.shape[1] * page
    page_table = page_table.astype(jnp.int32)

    cos_p, sin_p = _rope_tables128(jnp.arange(s, dtype=jnp.int32))
    rows_s = min(TOKEN_TILE, n_seq * t_new)
    cos_s, sin_s = _rope_tables128(past + (jnp.arange(rows_s, dtype=jnp.int32) % t_new))
    bias_diag, bias_sub = _prompt_bias_tiles(rel_bias)

    xp = x_prompt.reshape(b * s, D_MODEL)
    xs = x_sample.reshape(n_seq * t_new, D_MODEL)
    news = [[] for _ in range(8)]
    for l in range(depth):
        p = dict(norm_g=norm_g[l], w_in=w_in[l], da_q_norm=da_q_norm[l], da_k_norm=da_k_norm[l],
                 da_out_norm=da_out_norm[l], w_pa=w_pa[l], mla_cq_norm=mla_cq_norm[l], w_uq=w_uq[l],
                 mla_ckv_norm=mla_ckv_norm[l], w_ukv=w_ukv[l], mla_q_norm=mla_q_norm[l],
                 mla_k_norm=mla_k_norm[l], w_pb=w_pb[l], w_o=w_o[l])
        w = _prep_layer(p, t_new)
        lam_init = 0.8 - 0.6 * math.exp(-0.3 * l)
        lam = (jnp.exp(jnp.sum(lam_q1[l] * lam_k1[l]).astype(F32))
               - jnp.exp(jnp.sum(lam_q2[l] * lam_k2[l]).astype(F32)) + lam_init).reshape(1)

        qdt, kd, kdb, vd, vdt, c, kr, qmt, km, vmt = _project(xp, w, cos_p, sin_p, True, batch=b)
        oa_p = _diff_prompt(lam, qdt, kdb.reshape(b, s, 512), vdt, bias_diag, bias_sub)
        ob_p = _mla_prompt(qmt, km.reshape(b, s, MLA_HEADS * LANES), vmt)

        qd_s, kd_s, vd_s, c_s, kr_s, qm_s = _project(xs, w, cos_s, sin_s, False)
        qbd, qmn, qrr = _sample_query_layouts(qd_s, qm_s, n_seq, t_new)
        bias_tab, mla_mask, ctab, stab = _sample_tables(rel_bias, w['gk_rope'], past, t_new)
        new = dict(kd=kd_s.reshape(n_seq, t_new, DA_KV_HEADS * LANES), vd=vd_s.reshape(n_seq, t_new, DA_KV_HEADS * DA_V_DIM),
                   c=c_s.reshape(n_seq, t_new, MLA_KV_RANK),
                   krt=jnp.swapaxes(kr_s.reshape(n_seq, t_new, MLA_ROPE), 1, 2))
        oa_s, ob_s = _sample_attention(
            lam, page_table, cache_diff_k[l].reshape(pool, page * DA_KV_HEADS, LANES),
            cache_diff_v[l].reshape(pool, page * DA_KV_HEADS, DA_V_DIM), cache_mla_latent[l],
            jnp.swapaxes(cache_mla_krope[l], 1, 2), new, qbd, qmn, qrr, w, bias_tab, mla_mask, ctab, stab)

        xp = _merge(xp, oa_p.reshape(b * s, DA_WIDTH), ob_p.reshape(b * s, MLA_WIDTH), w, lam_init)
        xs = _merge(xs, oa_s.reshape(n_seq * t_new, DA_WIDTH), ob_s.reshape(n_seq * t_new, MLA_WIDTH), w, lam_init)

        for lst, a in zip(news, (kd.reshape(b, s, DA_KV_HEADS, LANES), vd.reshape(b, s, DA_KV_HEADS, DA_V_DIM),
                                 c.reshape(b, s, MLA_KV_RANK), kr.reshape(b, s, MLA_ROPE),
                                 kd_s.reshape(n_seq, t_new, DA_KV_HEADS, LANES),
                                 vd_s.reshape(n_seq, t_new, DA_KV_HEADS, DA_V_DIM),
                                 c_s.reshape(n_seq, t_new, MLA_KV_RANK), kr_s.reshape(n_seq, t_new, MLA_ROPE))):
            lst.append(a)
    return (xp.reshape(b, s, D_MODEL), xs.reshape(n_seq, t_new, D_MODEL)) + tuple(jnp.stack(a) for a in news)
```

```python
import functools
import math

import numpy as np
import jax
import jax.numpy as jnp
from jax import lax
from jax.experimental import pallas as pl
from jax.experimental.pallas import tpu as pltpu

F32 = jnp.float32
BF16 = jnp.bfloat16

D_MODEL = 1024
DA_HEADS = 8
DA_KV_HEADS = 4
DA_GROUP = DA_HEADS // DA_KV_HEADS
DA_HEAD_DIM = 64
DA_V_DIM = 2 * DA_HEAD_DIM
DA_WIDTH = DA_HEADS * DA_V_DIM
MLA_HEADS = 8
MLA_Q_RANK = 384
MLA_KV_RANK = 256
MLA_NOPE = 64
MLA_ROPE = 32
MLA_QK_DIM = MLA_NOPE + MLA_ROPE
MLA_V_DIM = 64
MLA_WIDTH = MLA_HEADS * MLA_V_DIM
ROPE_THETA = 10000.0
REL_BUCKETS = 32
REL_MAX_DIST = 128
EPS = 1e-6
IN_SPLITS = (DA_HEADS * 2 * DA_HEAD_DIM, DA_KV_HEADS * 2 * DA_HEAD_DIM, DA_KV_HEADS * DA_V_DIM, DA_WIDTH,
             MLA_Q_RANK, MLA_KV_RANK, MLA_ROPE, MLA_WIDTH, D_MODEL, D_MODEL)

LANES = 128
VMEM_LIMIT_BYTES = 56 * 1024 * 1024

MASKED = -1e30
HALF = MLA_ROPE // 2
ROPE_LO = MLA_NOPE
NOPE_HALF = MLA_NOPE // 2
SUM_ROWS = 16
LOG2E = math.log2(math.e)

_Q0, _K0, _V0, _CQ0, _CKV0, _KR0, _MAIN_COLS = 0, 1024, 1536, 2048, 2432, 2688, 2816

TOKEN_TILE = 512
ATT_TILE = 512
Q_CHUNK = 256
PAGES_PER_STEP = 8
PAGES_PER_BLOCK = 2


def _const_spec(shape):
    nd = len(shape)
    return pl.BlockSpec(shape, lambda *_: (0,) * nd, pipeline_mode=pl.Buffered(1))


def _dot(a, b):
    return jnp.dot(a, b, preferred_element_type=F32)


def _dot_tn(a, b):
    return lax.dot_general(a, b, (((0,), (0,)), ((), ())), preferred_element_type=F32)


def _lane_iota():
    return lax.broadcasted_iota(jnp.int32, (1, LANES), 1)


def _rope128(y, cos_t, sin_t):
    return y * cos_t + pltpu.roll(y, LANES - HALF, 1) * sin_t


def _project_body(prompt, x_ref, ng_ref, wm_ref, gq_ref, gk_ref, gcq_ref, wuq_ref, gmq_ref, gckv_ref,
                  wuk_ref, wuv_ref, gmk_ref, halves_ref, cos_ref, sin_ref, *out_refs):
    if prompt:
        qd_ref, kd_ref, kdb_ref, vd_ref, vdt_ref, c_ref, kr_ref, qm_ref, km_ref, vmt_ref = out_refs
    else:
        qd_ref, kd_ref, vd_ref, c_ref, kr_ref, qm_ref = out_refs
    x = x_ref[...]
    h = x * lax.rsqrt(jnp.mean(x * x, axis=-1, keepdims=True) + EPS) * ng_ref[...]
    hb = h.astype(BF16)
    live = (_lane_iota() < MLA_QK_DIM).astype(F32)
    cos_t = cos_ref[...]
    sin_t = sin_ref[...]
    halves = halves_ref[...]

    def lane_blocks(a):
        return [a[:, n * LANES:(n + 1) * LANES] for n in range(a.shape[1] // LANES)]

    def pair_norm(blks, g):
        sqs = [b * b for b in blks]
        his = [s.astype(BF16) for s in sqs]
        los = [(s - h_.astype(F32)).astype(BF16) for s, h_ in zip(sqs, his)]
        sums = [_dot(h_, halves) + _dot(l_, halves) for h_, l_ in zip(his, los)]
        return [b * lax.rsqrt(s / DA_HEAD_DIM + EPS) * g for b, s in zip(blks, sums)]

    def head_norm_rope(blks, g):
        sums = [jnp.sum(b * b * live, axis=-1, keepdims=True) for b in blks]
        normed = [b * lax.rsqrt(s / MLA_QK_DIM + EPS) * g for b, s in zip(blks, sums)]
        return [_rope128(y, cos_t, sin_t) for y in normed]

    def put_q(ref, vals):
        if prompt:
            outs = [jnp.transpose(v).astype(BF16) for v in vals]
            for n, o in enumerate(outs):
                ref[n * LANES:(n + 1) * LANES, :] = o
        else:
            for n, v in enumerate(vals):
                ref[:, n * LANES:(n + 1) * LANES] = v.astype(BF16)

    put_q(qd_ref, pair_norm(lane_blocks(_dot(hb, wm_ref[:, _Q0:_K0])), gq_ref[...]))
    kns = pair_norm(lane_blocks(_dot(hb, wm_ref[:, _K0:_V0])), gk_ref[...])
    va = _dot(hb, wm_ref[:, _V0:_CQ0])
    n_tok = x.shape[0]
    for hk, (kn, vh) in enumerate(zip(kns, lane_blocks(va))):
        kd_ref[pl.ds(hk, n_tok, stride=DA_KV_HEADS), :] = kn
        vd_ref[pl.ds(hk, n_tok, stride=DA_KV_HEADS), :] = vh
        if prompt:
            kdb_ref[:, hk * LANES:(hk + 1) * LANES] = kn.astype(BF16)
    if prompt:
        ones_rows = jnp.where(lax.broadcasted_iota(jnp.int32, (SUM_ROWS, x.shape[0]), 0) == 0, 1.0, 0.0).astype(BF16)
        for hk in range(DA_KV_HEADS):
            r0 = hk * (DA_V_DIM + SUM_ROWS)
            vdt_ref[r0:r0 + DA_V_DIM, :] = jnp.transpose(va[:, hk * LANES:(hk + 1) * LANES]).astype(BF16)
            vdt_ref[r0 + DA_V_DIM:r0 + DA_V_DIM + SUM_ROWS, :] = ones_rows

    cq = _dot(hb, wm_ref[:, _CQ0:_CKV0])
    cq = cq * lax.rsqrt(jnp.mean(cq * cq, axis=-1, keepdims=True) + EPS) * gcq_ref[...]
    put_q(qm_ref, head_norm_rope(lane_blocks(_dot(cq.astype(BF16), wuq_ref[...])), gmq_ref[...]))

    ckv = _dot(hb, wm_ref[:, _CKV0:_KR0])
    c = ckv * lax.rsqrt(jnp.mean(ckv * ckv, axis=-1, keepdims=True) + EPS) * gckv_ref[...]
    c_ref[...] = c
    krp = _dot(hb, wm_ref[:, _KR0:_MAIN_COLS])
    kr_ref[...] = krp[:, ROPE_LO:ROPE_LO + MLA_ROPE]

    if prompt:
        cb = c.astype(BF16)
        kn = _dot(cb, wuk_ref[...])
        kms = head_norm_rope([b + krp for b in lane_blocks(kn)], gmk_ref[...])
        for hm, km in enumerate(kms):
            km_ref[:, hm * LANES:(hm + 1) * LANES] = km.astype(BF16)
        vm = _dot(cb, wuv_ref[...])
        for blk in range(MLA_WIDTH // LANES):
            pair = jnp.transpose(vm[:, blk * LANES:(blk + 1) * LANES]).astype(BF16)
            for hh in range(2):
                r0 = (2 * blk + hh) * (MLA_V_DIM + SUM_ROWS)
                vmt_ref[r0:r0 + MLA_V_DIM, :] = pair[hh * MLA_V_DIM:(hh + 1) * MLA_V_DIM, :]
                vmt_ref[r0 + MLA_V_DIM:r0 + MLA_V_DIM + SUM_ROWS, :] = ones_rows


def _project(x2d, w, cos_tab, sin_tab, prompt, batch=None):
    n = x2d.shape[0]
    tm = min(TOKEN_TILE, n)
    assert n % tm == 0 and cos_tab.shape[0] % tm == 0
    n_pos = cos_tab.shape[0] // tm
    row = lambda cols: pl.BlockSpec((tm, cols), lambda i: (i, 0))
    pos = pl.BlockSpec((tm, LANES), lambda i: (i % n_pos, 0))
    consts = [w['norm_g'], w['w_main'], w['gq'], w['gk'], w['gcq'], w['w_uq'], w['gmq'], w['gckv'],
              w['w_uk_pad'], w['w_uv'], w['gmk'], w['halves']]
    sds = jax.ShapeDtypeStruct
    kv_shape = sds((n * DA_KV_HEADS, LANES), F32)
    kv_spec = pl.BlockSpec((tm * DA_KV_HEADS, LANES), lambda i: (i, 0))
    if prompt:
        s = cos_tab.shape[0]
        assert batch * s == n
        tr = lambda rows: pl.BlockSpec((None, rows, tm), lambda i: (i // n_pos, 0, i % n_pos))
        vdt_rows = DA_KV_HEADS * (DA_V_DIM + SUM_ROWS)
        vmt_rows = MLA_HEADS * (MLA_V_DIM + SUM_ROWS)
        out_shape = [sds((batch, DA_WIDTH, s), BF16), kv_shape, sds((n, 512), BF16), kv_shape,
                     sds((batch, vdt_rows, s), BF16), sds((n, MLA_KV_RANK), F32), sds((n, MLA_ROPE), F32),
                     sds((batch, MLA_HEADS * LANES, s), BF16), sds((n, MLA_HEADS * LANES), BF16),
                     sds((batch, vmt_rows, s), BF16)]
        out_specs = [tr(DA_WIDTH), kv_spec, row(512), kv_spec, tr(vdt_rows), row(MLA_KV_RANK), row(MLA_ROPE),
                     tr(MLA_HEADS * LANES), row(MLA_HEADS * LANES), tr(vmt_rows)]
    else:
        out_shape = [sds((n, DA_WIDTH), BF16), kv_shape, kv_shape, sds((n, MLA_KV_RANK), F32),
                     sds((n, MLA_ROPE), F32), sds((n, MLA_HEADS * LANES), BF16)]
        out_specs = [row(DA_WIDTH), kv_spec, kv_spec, row(MLA_KV_RANK), row(MLA_ROPE), row(MLA_HEADS * LANES)]
    return pl.pallas_call(
        functools.partial(_project_body, prompt),
        grid=(n // tm,),
        in_specs=[row(D_MODEL)] + [_const_spec(a.shape) for a in consts] + [pos, pos],
        out_specs=out_specs,
        out_shape=out_shape,
        compiler_params=pltpu.CompilerParams(dimension_semantics=("parallel",), vmem_limit_bytes=VMEM_LIMIT_BYTES),
        name="project_prompt" if prompt else "project_sample",
    )(x2d, *consts, cos_tab, sin_tab)


def _absorb_chunks(chunks, m_ref, acc_ref):
    stats = []
    for s, _, cols in chunks:
        m_old = m_ref[:, cols]
        m_new = jnp.maximum(m_old, jnp.max(s, axis=0, keepdims=True))
        m_ref[:, cols] = m_new
        stats.append((m_new, jnp.exp2(m_old - m_new)))
    probs = [jnp.exp2(s - m_new).astype(BF16) for (s, _, _), (m_new, _) in zip(chunks, stats)]
    for (_, vt, cols), (_, alpha), p in zip(chunks, stats, probs):
        acc_ref[:, cols] = alpha * acc_ref[:, cols] + _dot(vt, p)


def _key_tile_pipeline(i, qk, absorb):
    qk(0, 0)

    @pl.when(i == 0)
    def _():
        absorb(0, 0)

    @pl.when(i >= 1)
    def _():
        qk(1, 1)
        absorb(0, 0)

        @pl.when(i == 1)
        def _():
            absorb(1, 1)

        @pl.when(i >= 2)
        def _():
            qk(2, 0)
            absorb(1, 1)
            n_pairs = (i - 2) // 2

            def pair(p, carry):
                r = 2 + 2 * p
                qk(r + 1, 1)
                absorb(r, 0)
                qk(r + 2, 0)
                absorb(r + 1, 1)
                return carry
            lax.fori_loop(0, n_pairs, pair, 0)
            r = 2 + 2 * n_pairs

            @pl.when(r == i)
            def _():
                absorb(r, 0)

            @pl.when(r != i)
            def _():
                qk(r + 1, 1)
                absorb(r, 0)
                absorb(r + 1, 1)


def _diff_prompt_body(lam_ref, qt_ref, k_ref, vt_ref, bdiag_ref, bsub_ref, o_ref,
                      qs_ref, s0_ref, s1_ref, m_ref, acc_ref, o1_ref):
    i = pl.program_id(2)
    mp = pl.program_id(3)
    t = ATT_TILE
    ch = Q_CHUNK
    per_g = t // ch
    n_chunks = DA_GROUP * per_g
    s_bufs = (s0_ref, s1_ref)

    row = lax.broadcasted_iota(jnp.int32, (LANES, 1), 0)
    keep = (row < DA_HEAD_DIM) == (mp == 0)
    for g in range(DA_GROUP):
        qg = qt_ref[g * LANES:(g + 1) * LANES, :]
        qs_ref[:, g * t:(g + 1) * t] = jnp.where(keep, qg, jnp.zeros_like(qg))
    m_ref[...] = jnp.full(m_ref.shape, MASKED, F32)
    acc_ref[...] = jnp.zeros(acc_ref.shape, F32)

    def n_keys(r, cc):
        return (cc + 1) * ch if isinstance(r, int) and r == 0 else t

    def qk(r, buf):
        ks = pl.multiple_of((i - r) * t, t)
        for c in range(n_chunks):
            nk = n_keys(r, c % per_g)
            cols = slice(c * ch, (c + 1) * ch)
            s_bufs[buf][0:nk, cols] = _dot(k_ref[pl.ds(ks, nk), :], qs_ref[:, cols])

    def absorb(r, buf):
        ks = pl.multiple_of((i - r) * t, t)
        chunks = []
        for c in range(n_chunks):
            g, cc = divmod(c, per_g)
            nk = n_keys(r, cc)
            cols = slice(c * ch, (c + 1) * ch)
            s = s_bufs[buf][0:nk, cols]
            if isinstance(r, int) and r == 0:
                s = s + bdiag_ref[g, 0:nk, cc * ch:(cc + 1) * ch]
            elif isinstance(r, int) and r == 1 and cc == 0:
                s = s + bsub_ref[g]
            chunks.append((s, vt_ref[:, pl.ds(ks, nk)], cols))
        _absorb_chunks(chunks, m_ref, acc_ref)

    _key_tile_pipeline(i, qk, absorb)

    o = acc_ref[0:DA_V_DIM, :] * (1.0 / acc_ref[DA_V_DIM:DA_V_DIM + 1, :])

    @pl.when(mp == 0)
    def _():
        o1_ref[...] = o

    @pl.when(mp == 1)
    def _():
        comb = o1_ref[...] - lam_ref[0] * o
        for g in range(DA_GROUP):
            o_ref[:, g * LANES:(g + 1) * LANES] = jnp.transpose(comb[:, g * t:(g + 1) * t])


def _diff_prompt(lam, qdt, kdb, vdt, bias_diag, bias_sub):
    b, _, s = qdt.shape
    t = ATT_TILE
    assert s % t == 0
    vrows = DA_V_DIM + SUM_ROWS
    return pl.pallas_call(
        _diff_prompt_body,
        grid=(b, DA_KV_HEADS, s // t, 2),
        in_specs=[pl.BlockSpec(memory_space=pltpu.SMEM),
                  pl.BlockSpec((None, DA_GROUP * LANES, t), lambda bi, h, i, mp: (bi, h, i)),
                  pl.BlockSpec((None, s, LANES), lambda bi, h, i, mp: (bi, 0, h)),
                  pl.BlockSpec((None, vrows, s), lambda bi, h, i, mp: (bi, h, 0)),
                  pl.BlockSpec((DA_GROUP, t, t), lambda bi, h, i, mp: (h, 0, 0)),
                  pl.BlockSpec((DA_GROUP, t, Q_CHUNK), lambda bi, h, i, mp: (h, 0, 0))],
        out_specs=pl.BlockSpec((None, t, DA_GROUP * LANES), lambda bi, h, i, mp: (bi, i, h)),
        out_shape=jax.ShapeDtypeStruct((b, s, DA_WIDTH), F32),
        scratch_shapes=[pltpu.VMEM((LANES, DA_GROUP * t), BF16),
                        pltpu.VMEM((t, DA_GROUP * t), F32),
                        pltpu.VMEM((t, DA_GROUP * t), F32),
                        pltpu.VMEM((1, DA_GROUP * t), F32),
                        pltpu.VMEM((vrows, DA_GROUP * t), F32),
                        pltpu.VMEM((DA_V_DIM, DA_GROUP * t), F32)],
        compiler_params=pltpu.CompilerParams(
            dimension_semantics=("parallel", "parallel", "arbitrary", "arbitrary"), vmem_limit_bytes=VMEM_LIMIT_BYTES),
        name="diff_prompt",
    )(lam, qdt, kdb, vdt, bias_diag, bias_sub)


def _mla_prompt_body(qt_ref, k_ref, vt_ref, o_ref, s0_ref, s1_ref, m_ref, acc_ref, pair_ref):
    i = pl.program_id(2)
    hh = pl.program_id(3)
    t = ATT_TILE
    ch = Q_CHUNK
    n_chunks = t // ch
    scale2 = (MLA_QK_DIM ** -0.5) * LOG2E
    s_bufs = (s0_ref, s1_ref)
    m_ref[...] = jnp.full(m_ref.shape, MASKED, F32)
    acc_ref[...] = jnp.zeros(acc_ref.shape, F32)

    def n_keys(r, c):
        return (c + 1) * ch if isinstance(r, int) and r == 0 else t

    def qk(r, buf):
        ks = pl.multiple_of((i - r) * t, t)
        for c in range(n_chunks):
            nk = n_keys(r, c)
            cols = slice(c * ch, (c + 1) * ch)
            s_bufs[buf][0:nk, cols] = _dot(k_ref[pl.ds(ks, nk), :], qt_ref[:, cols])

    def absorb(r, buf):
        ks = pl.multiple_of((i - r) * t, t)
        chunks = []
        for c in range(n_chunks):
            nk = n_keys(r, c)
            cols = slice(c * ch, (c + 1) * ch)
            s = s_bufs[buf][0:nk, cols] * scale2
            if isinstance(r, int) and r == 0:
                kidx = lax.broadcasted_iota(jnp.int32, (nk, ch), 0)
                qidx = lax.broadcasted_iota(jnp.int32, (nk, ch), 1) + c * ch
                s = jnp.where(kidx <= qidx, s, MASKED)
            chunks.append((s, vt_ref[:, pl.ds(ks, nk)], cols))
        _absorb_chunks(chunks, m_ref, acc_ref)

    _key_tile_pipeline(i, qk, absorb)

    o = acc_ref[0:MLA_V_DIM, :] * (1.0 / acc_ref[MLA_V_DIM:MLA_V_DIM + 1, :])
    pair_ref[pl.ds(pl.multiple_of(hh * MLA_V_DIM, MLA_V_DIM), MLA_V_DIM), :] = o

    @pl.when(hh == 1)
    def _():
        o_ref[...] = jnp.transpose(pair_ref[...])


def _mla_prompt(qmt, km, vmt):
    b, _, s = qmt.shape
    t = ATT_TILE
    vrows = MLA_V_DIM + SUM_ROWS
    return pl.pallas_call(
        _mla_prompt_body,
        grid=(b, MLA_HEADS // 2, s // t, 2),
        in_specs=[pl.BlockSpec((None, LANES, t), lambda bi, hp, i, hh: (bi, 2 * hp + hh, i)),
                  pl.BlockSpec((None, s, LANES), lambda bi, hp, i, hh: (bi, 0, 2 * hp + hh)),
                  pl.BlockSpec((None, vrows, s), lambda bi, hp, i, hh: (bi, 2 * hp + hh, 0))],
        out_specs=pl.BlockSpec((None, t, 2 * MLA_V_DIM), lambda bi, hp, i, hh: (bi, i, hp)),
        out_shape=jax.ShapeDtypeStruct((b, s, MLA_WIDTH), F32),
        scratch_shapes=[pltpu.VMEM((t, t), F32), pltpu.VMEM((t, t), F32),
                        pltpu.VMEM((1, t), F32),
                        pltpu.VMEM((vrows, t), F32),
                        pltpu.VMEM((2 * MLA_V_DIM, t), F32)],
        compiler_params=pltpu.CompilerParams(
            dimension_semantics=("parallel", "parallel", "arbitrary", "arbitrary"), vmem_limit_bytes=VMEM_LIMIT_BYTES),
        name="mla_prompt",
    )(qmt, km, vmt)


def _sample_body(n_steps, pt_ref, lam_ref, ck_hbm, cv_hbm, cc_hbm, cr_hbm,
                 kn_ref, vn_ref, cn_ref, rn_ref, qbd_ref, qmn_ref, qrr_ref, gkn_ref, wuk_ref, wuv_ref, eseg_ref,
                 bias_ref, mmask_ref, ctab_ref, stab_ref,
                 oa_ref, ob_ref,
                 qabs_ref, md_ref, ld_ref, accd_ref, mm_ref, lm_ref, accm_ref, kbuf, vbuf, cbuf, rbuf, sems):
    pg = PAGES_PER_STEP
    b = pl.program_id(0)
    j = pl.program_id(1)
    scale_m = (MLA_QK_DIM ** -0.5) * LOG2E
    page = cc_hbm.shape[1]
    krows = page * DA_KV_HEADS
    t_new = cn_ref.shape[0]
    n_mla = MLA_HEADS * t_new

    def page_copies(pid, p, slot):
        return (pltpu.make_async_copy(ck_hbm.at[pid], kbuf.at[slot, pl.ds(p * krows, krows), :], sems.at[slot, 0]),
                pltpu.make_async_copy(cv_hbm.at[pid], vbuf.at[slot, pl.ds(p * krows, krows), :], sems.at[slot, 1]),
                pltpu.make_async_copy(cc_hbm.at[pid], cbuf.at[slot, pl.ds(p * page, page), :], sems.at[slot, 2]),
                pltpu.make_async_copy(cr_hbm.at[pid], rbuf.at[slot, :, pl.ds(p * page, page)], sems.at[slot, 3]))

    def start_step(bb, jj, slot):
        for p in range(pg):
            for cp in page_copies(pt_ref[bb, jj * pg + p], p, slot):
                cp.start()

    def wait_step(slot):
        for p in range(pg):
            for cp in page_copies(0, p, slot):
                cp.wait()

    g = b * n_steps + j
    slot = lax.rem(g, 2)

    @pl.when(g == 0)
    def _():
        start_step(0, 0, 0)

    @pl.when(g + 1 < pl.num_programs(0) * n_steps)
    def _():
        nxt = g + 1
        start_step(nxt // n_steps, lax.rem(nxt, n_steps), 1 - slot)

    wait_step(slot)

    def heads_to_lanes(ref2, start=0, t=None):
        t = ref2.shape[0] // DA_KV_HEADS if t is None else t
        return jnp.concatenate([ref2[pl.ds(start + h, t, stride=DA_KV_HEADS), :].astype(BF16)
                                for h in range(DA_KV_HEADS)], axis=1)

    def rows_from_lanes(alpha, n_rows):
        full = jnp.broadcast_to(alpha, (LANES, alpha.shape[1]))
        if alpha.shape[1] < LANES:
            full = jnp.concatenate([full, jnp.zeros((LANES, LANES - alpha.shape[1]), F32)], axis=1)
        return jnp.transpose(full)[:n_rows, :]

    def score_matmuls(kb, c, krt, ctab, stab):
        cb = c.astype(BF16)
        s = _dot(kb, qbd_ref[...])
        kn = _dot(cb, wuk_ref[...])
        raw = _dot(cb, qabs_ref[...])
        ext = _dot_tn(jnp.concatenate([krt * ctab, krt * stab, krt * krt], axis=0).astype(BF16), qrr_ref[...])
        return cb, s, kn, raw, ext

    def local_softmaxes(blocks):
        half = MLA_HEADS * NOPE_HALF
        ss = []
        for (cb, s, kn, raw, ext), _, _, _ in blocks:
            kn2 = kn * kn
            ss.append(_dot((kn2[:, :half] + kn2[:, half:]).astype(BF16), eseg_ref[...]))
        sa = [s + bias for (_, s, _, _, _), _, bias, _ in blocks]
        ma = [jnp.max(s, axis=0, keepdims=True) for s in sa]
        pa = [jnp.exp2(s - m) for s, m in zip(sa, ma)]
        sb = []
        for ((_, _, _, raw, ext), _, _, mmask), q in zip(blocks, ss):
            r = lax.rsqrt((q + pltpu.roll(ext, LANES - n_mla, 1)[:, :n_mla]) / MLA_QK_DIM + EPS)
            sm = (raw + ext[:, :n_mla]) * r * scale_m
            sb.append(sm if mmask is None else sm + mmask)
        mb = [jnp.max(s, axis=0, keepdims=True) for s in sb]
        pb_ = [jnp.exp2(s - m) for s, m in zip(sb, mb)]
        parts_a = [(m, jnp.sum(p, axis=0, keepdims=True), _dot_tn(p.astype(BF16), vb))
                   for m, p, (_, vb, _, _) in zip(ma, pa, blocks)]
        parts_b = [(m, jnp.sum(p, axis=0, keepdims=True), _dot_tn(p.astype(BF16), pre[0]))
                   for m, p, (pre, _, _, _) in zip(mb, pb_, blocks)]
        return parts_a, parts_b

    def fold(m_ref, l_ref, acc_ref, parts):
        n_rows, width = acc_ref.shape
        m_old = m_ref[...]
        m_new = m_old
        for m, _, _ in parts:
            m_new = jnp.maximum(m_new, m)
        wide = lambda f: jnp.concatenate([rows_from_lanes(f, n_rows)] * (width // LANES), axis=1)
        alpha = jnp.exp2(m_old - m_new)
        l_new = alpha * l_ref[...]
        acc = wide(alpha) * acc_ref[...]
        for m, l, a in parts:
            f = jnp.exp2(m - m_new)
            l_new = l_new + f * l
            acc = acc + wide(f) * a
        m_ref[...] = m_new
        l_ref[...] = l_new
        acc_ref[...] = acc

    @pl.when(j == 0)
    def _():
        qg = (qmn_ref[...].astype(F32) * gkn_ref[...]).astype(BF16)
        qabs_ref[...] = _dot(wuk_ref[...], qg).astype(BF16)
        past = n_steps * pg * page
        rows_n = pl.ds(past, t_new)
        pre = score_matmuls(heads_to_lanes(kn_ref), cn_ref[...], rn_ref[...], ctab_ref[:, rows_n], stab_ref[:, rows_n])
        ((m_d, l_d, pv),), ((m_m, l_m, pc),) = local_softmaxes(
            [(pre, heads_to_lanes(vn_ref), bias_ref[rows_n, :], mmask_ref[...])])
        md_ref[...] = m_d
        ld_ref[...] = l_d
        accd_ref[...] = pv
        mm_ref[...] = m_m
        lm_ref[...] = l_m
        accm_ref[...] = pc

    base = j * (pg * page)
    pb = PAGES_PER_BLOCK
    blocks = []
    tb = pb * page
    for blk in range(pg // pb):
        rows = pl.ds(pl.multiple_of(base + blk * tb, tb), tb)
        keys = slice(blk * tb, (blk + 1) * tb)
        pre = score_matmuls(heads_to_lanes(kbuf.at[slot], blk * tb * DA_KV_HEADS, tb), cbuf[slot, keys, :],
                            rbuf[slot, :, keys], ctab_ref[:, rows], stab_ref[:, rows])
        blocks.append((pre, heads_to_lanes(vbuf.at[slot], blk * tb * DA_KV_HEADS, tb), bias_ref[rows, :], None))
    parts_d, parts_m = local_softmaxes(blocks)
    fold(md_ref, ld_ref, accd_ref, parts_d)
    fold(mm_ref, lm_ref, accm_ref, parts_m)

    @pl.when(j == n_steps - 1)
    def _():
        lam = lam_ref[0]
        inv_d = rows_from_lanes(1.0 / ld_ref[...], LANES)
        accd = accd_ref[...]
        lanes_per_kv = 2 * DA_GROUP * t_new
        for h in range(DA_KV_HEADS):
            cols = slice(h * LANES, (h + 1) * LANES)
            for g in range(DA_GROUP):
                r1 = h * lanes_per_kv + g * t_new
                r2 = r1 + DA_GROUP * t_new
                o1 = accd[r1:r1 + t_new, cols] * inv_d[r1:r1 + t_new, :]
                o2 = accd[r2:r2 + t_new, cols] * inv_d[r2:r2 + t_new, :]
                hq = h * DA_GROUP + g
                oa_ref[:, hq * LANES:(hq + 1) * LANES] = o1 - lam * o2
        inv_m = rows_from_lanes(1.0 / lm_ref[...], n_mla)
        cbar = (accm_ref[...] * jnp.concatenate([inv_m] * (MLA_KV_RANK // LANES), axis=1)).astype(BF16)
        full = _dot(cbar, wuv_ref[...])
        lane5 = lax.broadcasted_iota(jnp.int32, (1, MLA_WIDTH), 1)
        ob = jnp.zeros((t_new, MLA_WIDTH), F32)
        for h in range(MLA_HEADS):
            sel = (lane5 >= h * MLA_V_DIM) & (lane5 < (h + 1) * MLA_V_DIM)
            ob = ob + jnp.where(sel, full[h * t_new:(h + 1) * t_new, :], 0.0)
        ob_ref[...] = ob


def _sample_attention(lam, page_table, ck, cv, cc, crt, new, qbd, qmn, qrr, w, bias_tab, mla_mask, ctab, stab):
    n_seq, n_pages = page_table.shape
    pool, page = cc.shape[0], cc.shape[1]
    t_new = new['c'].shape[1]
    pg = PAGES_PER_STEP
    assert n_pages % pg == 0 and pg % PAGES_PER_BLOCK == 0 and MLA_HEADS * t_new <= LANES // 2
    n_steps = n_pages // pg

    def seq_spec(*tail):
        zeros = (0,) * len(tail)
        return pl.BlockSpec((None,) + tail, lambda b, j, pt: (b,) + zeros)

    def const(shape):
        nd = len(shape)
        return pl.BlockSpec(shape, lambda b, j, pt: (0,) * nd, pipeline_mode=pl.Buffered(1))

    in_specs = [pl.BlockSpec(memory_space=pltpu.SMEM)] + [pl.BlockSpec(memory_space=pl.ANY)] * 4
    per_seq = [new['kd'], new['vd'], new['c'], new['krt'], qbd, qmn, qrr]
    in_specs += [seq_spec(*a.shape[1:]) for a in per_seq]
    consts = [w['gk_nope_col'], w['w_uk'], w['w_uv'], w['eseg'], bias_tab, mla_mask, ctab, stab]
    in_specs += [const(a.shape) for a in consts]
    operands = [ck, cv, cc, crt] + per_seq + consts
    krows = page * DA_KV_HEADS
    grid_spec = pltpu.PrefetchScalarGridSpec(
        num_scalar_prefetch=1,
        grid=(n_seq, n_steps),
        in_specs=in_specs,
        out_specs=[pl.BlockSpec((None, t_new, DA_WIDTH), lambda b, j, pt: (b, 0, 0)),
                   pl.BlockSpec((None, t_new, MLA_WIDTH), lambda b, j, pt: (b, 0, 0))],
        scratch_shapes=[pltpu.VMEM((MLA_KV_RANK, MLA_HEADS * t_new), BF16),
                        pltpu.VMEM((1, LANES), F32), pltpu.VMEM((1, LANES), F32),
                        pltpu.VMEM((LANES, DA_KV_HEADS * DA_V_DIM), F32),
                        pltpu.VMEM((1, MLA_HEADS * t_new), F32), pltpu.VMEM((1, MLA_HEADS * t_new), F32),
                        pltpu.VMEM((MLA_HEADS * t_new, MLA_KV_RANK), F32),
                        pltpu.VMEM((2, pg * krows, LANES), F32),
                        pltpu.VMEM((2, pg * krows, DA_V_DIM), F32),
                        pltpu.VMEM((2, pg * page, MLA_KV_RANK), F32),
                        pltpu.VMEM((2, MLA_ROPE, pg * page), F32),
                        pltpu.SemaphoreType.DMA((2, 4))])
    return pl.pallas_call(
        functools.partial(_sample_body, n_steps),
        grid_spec=grid_spec,
        out_shape=[jax.ShapeDtypeStruct((n_seq, t_new, DA_WIDTH), F32),
                   jax.ShapeDtypeStruct((n_seq, t_new, MLA_WIDTH), F32)],
        compiler_params=pltpu.CompilerParams(dimension_semantics=("arbitrary", "arbitrary"),
                                             vmem_limit_bytes=VMEM_LIMIT_BYTES),
        name="sample_attention",
    )(page_table, lam, *operands)


def _merge_body(one_minus_lam_init, x_ref, oa_ref, ob_ref, ng_ref, wzg_ref, gout_ref, wpa_ref, wpb_ref, wo_ref, y_ref):
    x = x_ref[...]
    h = x * lax.rsqrt(jnp.mean(x * x, axis=-1, keepdims=True) + EPS) * ng_ref[...]
    hb = h.astype(BF16)
    za = _dot(hb, wzg_ref[:, 0:DA_WIDTH])
    gout = gout_ref[...]
    parts = []
    for hq in range(DA_HEADS):
        sl = slice(hq * LANES, (hq + 1) * LANES)
        o = oa_ref[:, sl]
        o = o * lax.rsqrt(jnp.mean(o * o, axis=-1, keepdims=True) + EPS) * gout * one_minus_lam_init
        z = za[:, sl]
        parts.append((o * (z * jax.nn.sigmoid(z))).astype(BF16))
    ya = _dot(jnp.concatenate(parts, axis=1), wpa_ref[...])
    zb = _dot(hb, wzg_ref[:, DA_WIDTH:DA_WIDTH + MLA_WIDTH])
    yb = _dot((ob_ref[...] * (zb * jax.nn.sigmoid(zb))).astype(BF16), wpb_ref[...])
    g0 = DA_WIDTH + MLA_WIDTH
    ga = _dot(hb, wzg_ref[:, g0:g0 + D_MODEL])
    gb = _dot(hb, wzg_ref[:, g0 + D_MODEL:g0 + 2 * D_MODEL])
    mix = jax.nn.sigmoid(ga) * ya + jax.nn.sigmoid(gb) * yb
    y_ref[...] = x + _dot(mix.astype(BF16), wo_ref[...])


def _merge(x2d, oa, ob, w, lam_init):
    n = x2d.shape[0]
    tm = min(TOKEN_TILE, n)
    assert n % tm == 0
    row = lambda cols: pl.BlockSpec((tm, cols), lambda i: (i, 0))
    consts = [w['norm_g'], w['w_zg'], w['gout'], w['w_pa'], w['w_pb'], w['w_o']]
    return pl.pallas_call(
        functools.partial(_merge_body, 1.0 - lam_init),
        grid=(n // tm,),
        in_specs=[row(D_MODEL), row(DA_WIDTH), row(MLA_WIDTH)] + [_const_spec(a.shape) for a in consts],
        out_specs=row(D_MODEL),
        out_shape=jax.ShapeDtypeStruct((n, D_MODEL), F32),
        compiler_params=pltpu.CompilerParams(dimension_semantics=("parallel",), vmem_limit_bytes=VMEM_LIMIT_BYTES),
        name="merge",
    )(x2d, oa, ob, *consts)


def _rel_bucket(dist):
    n = jnp.maximum(dist, 0)
    max_exact = REL_BUCKETS // 2
    nf = jnp.maximum(n, 1).astype(F32)
    large = max_exact + (jnp.log(nf / max_exact) / math.log(REL_MAX_DIST / max_exact)
                         * (REL_BUCKETS - max_exact)).astype(jnp.int32)
    large = jnp.minimum(large, REL_BUCKETS - 1)
    return jnp.where(n < max_exact, n, large)


def _rope_angles(pos):
    inv = jnp.power(ROPE_THETA, -jnp.arange(HALF, dtype=F32) / HALF)
    ang = pos.astype(F32)[:, None] * inv[None, :]
    return jnp.cos(ang), jnp.sin(ang)


def _rope_tables128(pos):
    cos, sin = _rope_angles(pos)
    n = pos.shape[0]
    cos_t = jnp.concatenate([jnp.ones((n, MLA_NOPE), F32), cos, cos, jnp.zeros((n, LANES - MLA_QK_DIM), F32)], axis=1)
    sin_t = jnp.concatenate([jnp.zeros((n, MLA_NOPE), F32), -sin, sin, jnp.zeros((n, LANES - MLA_QK_DIM), F32)], axis=1)
    return cos_t, sin_t


def _pad_heads(a, live):
    pad = [(0, 0)] * (a.ndim - 1) + [(0, LANES - live)]
    a = jnp.pad(a, pad)
    return a.reshape(a.shape[:-2] + (a.shape[-2] * LANES,))


def _prep_layer(p, t_new):
    offs = [int(v) for v in np.cumsum((0,) + IN_SPLITS)]
    wq, wk, wv, wza, wcq, wckv, wkr, wzb, wga, wgb = [p['w_in'][:, offs[n]:offs[n + 1]] for n in range(10)]
    with_x1_copy = lambda a: jnp.concatenate([a, a[..., MLA_NOPE:MLA_NOPE + HALF]], axis=-1)
    tail = LANES - MLA_QK_DIM - HALF
    wkr_placed = jnp.pad(jnp.concatenate([wkr, wkr[:, :HALF]], axis=1), ((0, 0), (ROPE_LO, tail)))
    halves = np.arange(LANES) // DA_HEAD_DIM
    w_ukv = p['w_ukv'].reshape(MLA_KV_RANK, MLA_HEADS, MLA_NOPE + MLA_V_DIM)
    w_uk = w_ukv[:, :, :MLA_NOPE]
    w_uk_halves = jnp.transpose(w_uk.reshape(MLA_KV_RANK, MLA_HEADS, 2, NOPE_HALF), (0, 2, 1, 3))
    gmk = p['mla_k_norm']
    gk_nope = jnp.broadcast_to(gmk[:MLA_NOPE].reshape(2, 1, NOPE_HALF), (2, MLA_HEADS, NOPE_HALF))
    head_of_row = np.arange(MLA_HEADS * NOPE_HALF) // NOPE_HALF
    head_of_lane = np.arange(MLA_HEADS * t_new) // t_new
    return dict(
        norm_g=p['norm_g'].reshape(1, D_MODEL),
        w_main=jnp.concatenate([wq, wk, wv, wcq, wckv, wkr_placed], axis=1).astype(BF16),
        w_zg=jnp.concatenate([wza, wzb, wga, wgb], axis=1).astype(BF16),
        gq=p['da_q_norm'].reshape(1, LANES) * ((DA_HEAD_DIM ** -0.5) * LOG2E),
        gk=p['da_k_norm'].reshape(1, LANES),
        gcq=p['mla_cq_norm'].reshape(1, MLA_Q_RANK),
        w_uq=_pad_heads(with_x1_copy(p['w_uq'].reshape(MLA_Q_RANK, MLA_HEADS, MLA_QK_DIM)),
                        MLA_QK_DIM + HALF).astype(BF16),
        gmq=jnp.pad(with_x1_copy(p['mla_q_norm']), (0, tail)).reshape(1, LANES),
        halves=jnp.asarray(halves[:, None] == halves[None, :], BF16),
        gckv=p['mla_ckv_norm'].reshape(1, MLA_KV_RANK),
        w_uk_pad=_pad_heads(w_uk, MLA_NOPE).astype(BF16),
        w_uk=w_uk_halves.reshape(MLA_KV_RANK, MLA_HEADS * MLA_NOPE).astype(BF16),
        w_uv=w_ukv[:, :, MLA_NOPE:].reshape(MLA_KV_RANK, MLA_WIDTH).astype(BF16),
        gmk=jnp.pad(with_x1_copy(gmk), (0, tail)).reshape(1, LANES),
        gk_nope_col=gk_nope.reshape(MLA_HEADS * MLA_NOPE, 1),
        gk_rope=gmk[MLA_NOPE:],
        eseg=jnp.asarray(head_of_row[:, None] == head_of_lane[None, :], BF16),
        gout=p['da_out_norm'].reshape(1, DA_V_DIM),
        w_pa=p['w_pa'].astype(BF16), w_pb=p['w_pb'].astype(BF16), w_o=p['w_o'].astype(BF16),
    )


def _toeplitz(v, n):
    lead = v.shape[:-1]
    u = jnp.flip(v, axis=-1)
    w = jnp.concatenate([u, jnp.zeros(lead + (1,), v.dtype)], axis=-1)
    flat = jnp.tile(w, (1,) * len(lead) + (n,))[..., :n * (2 * n - 1)]
    return flat.reshape(lead + (n, 2 * n - 1))[..., n - 1:]


def _prompt_bias_tiles(rel_bias):
    t = ATT_TILE
    far = rel_bias[REL_BUCKETS - 1]
    d = jnp.arange(2 * t - 1, dtype=jnp.int32) - (t - 1)
    f = jnp.where((d >= 0)[:, None], rel_bias[_rel_bucket(d)] - far[None, :], MASKED)
    diag = _toeplitz(f.T, t)
    dc = LANES + jnp.arange(2 * LANES - 1, dtype=jnp.int32) - (LANES - 1)
    corner = _toeplitz((rel_bias[_rel_bucket(dc)] - far[None, :]).T, LANES)
    sub = jnp.pad(jnp.swapaxes(corner, 1, 2), ((0, 0), (t - LANES, 0), (0, Q_CHUNK - LANES)))
    return jnp.swapaxes(diag, 1, 2) * LOG2E, sub * LOG2E


def _sample_tables(rel_bias, gk_rope, past, t_new):
    n_keys = past + t_new
    near = REL_MAX_DIST + t_new
    k_pos = jnp.arange(n_keys - near, n_keys, dtype=jnp.int32)
    q_pos = past + jnp.arange(t_new, dtype=jnp.int32)
    b = rel_bias[_rel_bucket(q_pos[None, :] - k_pos[:, None])]
    b = b - rel_bias[REL_BUCKETS - 1][None, None, :]
    b = jnp.where((q_pos[None, :] >= k_pos[:, None])[:, :, None], b, MASKED)
    b = jnp.transpose(b.reshape(near, t_new, DA_KV_HEADS, DA_GROUP), (0, 2, 3, 1))
    b = jnp.broadcast_to(b[:, :, None], (near, DA_KV_HEADS, 2, DA_GROUP, t_new)).reshape(near, 2 * DA_HEADS * t_new)
    b = jnp.pad(b.astype(F32) * LOG2E, ((n_keys - near, 0), (0, 0)))
    newk = jnp.arange(t_new)
    mm = jnp.where(newk[None, :] >= newk[:, None], 0.0, MASKED).astype(F32)
    mm = jnp.tile(mm, (1, MLA_HEADS))
    cos, sin = _rope_angles(jnp.arange(n_keys, dtype=jnp.int32))
    ctab = jnp.concatenate([cos, cos], axis=1) * gk_rope[None, :]
    stab = jnp.concatenate([sin, sin], axis=1) * gk_rope[None, :]
    return b, mm, ctab.T, stab.T


def _sample_query_layouts(qd, qm, n_seq, t_new):
    q = qd.reshape(n_seq, t_new, DA_KV_HEADS, DA_GROUP, 2, DA_HEAD_DIM)
    q = jnp.transpose(q, (0, 2, 4, 5, 3, 1)).reshape(n_seq, DA_KV_HEADS * 2, DA_HEAD_DIM, DA_GROUP * t_new)
    eye = jnp.eye(DA_KV_HEADS * 2, dtype=qd.dtype)
    qbd = q[:, :, :, None, :] * eye[None, :, None, :, None]
    qbd = qbd.reshape(n_seq, DA_KV_HEADS * 2 * DA_HEAD_DIM, DA_KV_HEADS * 2 * DA_GROUP * t_new)
    qh = qm.reshape(n_seq, t_new, MLA_HEADS, LANES)
    qn = jnp.transpose(qh[..., :MLA_NOPE].reshape(n_seq, t_new, MLA_HEADS, 2, NOPE_HALF), (0, 3, 2, 4, 1))
    eye_h = jnp.eye(MLA_HEADS, dtype=qm.dtype)
    qmn = (qn[:, :, :, :, None, :] * eye_h[None, None, :, None, :, None]).reshape(
        n_seq, MLA_HEADS * MLA_NOPE, MLA_HEADS * t_new)
    qr = jnp.transpose(qh[..., ROPE_LO:ROPE_LO + MLA_ROPE], (0, 3, 2, 1)).reshape(n_seq, MLA_ROPE, MLA_HEADS * t_new)
    qrs = jnp.concatenate([qr[:, HALF:], -qr[:, :HALF]], axis=1)
    n_mla = MLA_HEADS * t_new
    top = jnp.pad(jnp.concatenate([qr, qrs], axis=1), ((0, 0), (0, 0), (0, LANES - n_mla)))
    bottom = jnp.pad(jnp.ones((n_seq, MLA_ROPE, n_mla), qm.dtype), ((0, 0), (0, 0), (LANES - n_mla, 0)))
    return qbd, qmn, jnp.concatenate([top, bottom], axis=1)


def kernel(x_prompt, x_sample, cache_diff_k, cache_diff_v, cache_mla_latent, cache_mla_krope, page_table, rel_bias,
           norm_g, w_in, da_q_norm, da_k_norm, lam_q1, lam_k1, lam_q2, lam_k2, da_out_norm, w_pa, mla_cq_norm, w_uq,
           mla_ckv_norm, w_ukv, mla_q_norm, mla_k_norm, w_pb, w_o):
    b, s, _ = x_prompt.shape
    n_seq, t_new, _ = x_sample.shape
    depth, pool, page = cache_diff_k.shape[:3]
    past = page_table.shape[1] * page
    page_table = page_table.astype(jnp.int32)

    cos_p, sin_p = _rope_tables128(jnp.arange(s, dtype=jnp.int32))
    rows_s = min(TOKEN_TILE, n_seq * t_new)
    cos_s, sin_s = _rope_tables128(past + (jnp.arange(rows_s, dtype=jnp.int32) % t_new))
    bias_diag, bias_sub = _prompt_bias_tiles(rel_bias)

    xp = x_prompt.reshape(b * s, D_MODEL)
    xs = x_sample.reshape(n_seq * t_new, D_MODEL)
    news = [[] for _ in range(8)]
    for l in range(depth):
        p = dict(norm_g=norm_g[l], w_in=w_in[l], da_q_norm=da_q_norm[l], da_k_norm=da_k_norm[l],
                 da_out_norm=da_out_norm[l], w_pa=w_pa[l], mla_cq_norm=mla_cq_norm[l], w_uq=w_uq[l],
                 mla_ckv_norm=mla_ckv_norm[l], w_ukv=w_ukv[l], mla_q_norm=mla_q_norm[l],
                 mla_k_norm=mla_k_norm[l], w_pb=w_pb[l], w_o=w_o[l])
        w = _prep_layer(p, t_new)
        lam_init = 0.8 - 0.6 * math.exp(-0.3 * l)
        lam = (jnp.exp(jnp.sum(lam_q1[l] * lam_k1[l]).astype(F32))
               - jnp.exp(jnp.sum(lam_q2[l] * lam_k2[l]).astype(F32)) + lam_init).reshape(1)

        qdt, kd, kdb, vd, vdt, c, kr, qmt, km, vmt = _project(xp, w, cos_p, sin_p, True, batch=b)
        oa_p = _diff_prompt(lam, qdt, kdb.reshape(b, s, 512), vdt, bias_diag, bias_sub)
        ob_p = _mla_prompt(qmt, km.reshape(b, s, MLA_HEADS * LANES), vmt)

        qd_s, kd_s, vd_s, c_s, kr_s, qm_s = _project(xs, w, cos_s, sin_s, False)
        qbd, qmn, qrr = _sample_query_layouts(qd_s, qm_s, n_seq, t_new)
        bias_tab, mla_mask, ctab, stab = _sample_tables(rel_bias, w['gk_rope'], past, t_new)
        new = dict(kd=kd_s.reshape(n_seq, t_new * DA_KV_HEADS, LANES), vd=vd_s.reshape(n_seq, t_new * DA_KV_HEADS, DA_V_DIM),
                   c=c_s.reshape(n_seq, t_new, MLA_KV_RANK),
                   krt=jnp.swapaxes(kr_s.reshape(n_seq, t_new, MLA_ROPE), 1, 2))
        oa_s, ob_s = _sample_attention(
            lam, page_table, cache_diff_k[l].reshape(pool, page * DA_KV_HEADS, LANES),
            cache_diff_v[l].reshape(pool, page * DA_KV_HEADS, DA_V_DIM), cache_mla_latent[l],
            jnp.swapaxes(cache_mla_krope[l], 1, 2), new, qbd, qmn, qrr, w, bias_tab, mla_mask, ctab, stab)

        xp = _merge(xp, oa_p.reshape(b * s, DA_WIDTH), ob_p.reshape(b * s, MLA_WIDTH), w, lam_init)
        xs = _merge(xs, oa_s.reshape(n_seq * t_new, DA_WIDTH), ob_s.reshape(n_seq * t_new, MLA_WIDTH), w, lam_init)

        for lst, a in zip(news, (kd.reshape(b, s, DA_KV_HEADS, LANES), vd.reshape(b, s, DA_KV_HEADS, DA_V_DIM),
                                 c.reshape(b, s, MLA_KV_RANK), kr.reshape(b, s, MLA_ROPE),
                                 kd_s.reshape(n_seq, t_new, DA_KV_HEADS, LANES),
                                 vd_s.reshape(n_seq, t_new, DA_KV_HEADS, DA_V_DIM),
                                 c_s.reshape(n_seq, t_new, MLA_KV_RANK), kr_s.reshape(n_seq, t_new, MLA_ROPE))):
            lst.append(a)
    return (xp.reshape(b, s, D_MODEL), xs.reshape(n_seq, t_new, D_MODEL)) + tuple(jnp.stack(a) for a in news)
```

```python
import functools
import math

import numpy as np
import jax
import jax.numpy as jnp
from jax import lax
from jax.experimental import pallas as pl
from jax.experimental.pallas import tpu as pltpu

F32 = jnp.float32
BF16 = jnp.bfloat16

D_MODEL = 1024
DA_HEADS = 8
DA_KV_HEADS = 4
DA_GROUP = DA_HEADS // DA_KV_HEADS
DA_HEAD_DIM = 64
DA_V_DIM = 2 * DA_HEAD_DIM
DA_WIDTH = DA_HEADS * DA_V_DIM
MLA_HEADS = 8
MLA_Q_RANK = 384
MLA_KV_RANK = 256
MLA_NOPE = 64
MLA_ROPE = 32
MLA_QK_DIM = MLA_NOPE + MLA_ROPE
MLA_V_DIM = 64
MLA_WIDTH = MLA_HEADS * MLA_V_DIM
ROPE_THETA = 10000.0
REL_BUCKETS = 32
REL_MAX_DIST = 128
EPS = 1e-6
IN_SPLITS = (DA_HEADS * 2 * DA_HEAD_DIM, DA_KV_HEADS * 2 * DA_HEAD_DIM, DA_KV_HEADS * DA_V_DIM, DA_WIDTH,
             MLA_Q_RANK, MLA_KV_RANK, MLA_ROPE, MLA_WIDTH, D_MODEL, D_MODEL)

LANES = 128
VMEM_LIMIT_BYTES = 56 * 1024 * 1024

MASKED = -1e30
HALF = MLA_ROPE // 2
ROPE_LO = MLA_NOPE
NOPE_HALF = MLA_NOPE // 2
SUM_ROWS = 16
LOG2E = math.log2(math.e)

_Q0, _K0, _V0, _CQ0, _CKV0, _KR0, _MAIN_COLS = 0, 1024, 1536, 2048, 2432, 2688, 2816

TOKEN_TILE = 512
ATT_TILE = 512
Q_CHUNK = 256
PAGES_PER_STEP = 8
PAGES_PER_BLOCK = 2
PAGE_SLOTS = 3


def _const_spec(shape):
    nd = len(shape)
    return pl.BlockSpec(shape, lambda *_: (0,) * nd, pipeline_mode=pl.Buffered(1))


def _dot(a, b):
    return jnp.dot(a, b, preferred_element_type=F32)


def _dot_tn(a, b):
    return lax.dot_general(a, b, (((0,), (0,)), ((), ())), preferred_element_type=F32)


def _lane_iota():
    return lax.broadcasted_iota(jnp.int32, (1, LANES), 1)


def _rope128(y, cos_t, sin_t):
    return y * cos_t + pltpu.roll(y, LANES - HALF, 1) * sin_t


def _project_body(prompt, x_ref, ng_ref, wm_ref, gq_ref, gk_ref, gcq_ref, wuq_ref, gmq_ref, gckv_ref,
                  wuk_ref, wuv_ref, gmk_ref, halves_ref, cos_ref, sin_ref, *out_refs):
    if prompt:
        qd_ref, kd_ref, kdb_ref, vd_ref, vdt_ref, c_ref, kr_ref, qm_ref, km_ref, vmt_ref = out_refs
    else:
        qd_ref, kd_ref, vd_ref, c_ref, kr_ref, qm_ref = out_refs
    x = x_ref[...]
    h = x * lax.rsqrt(jnp.mean(x * x, axis=-1, keepdims=True) + EPS) * ng_ref[...]
    hb = h.astype(BF16)
    live = (_lane_iota() < MLA_QK_DIM).astype(F32)
    cos_t = cos_ref[...]
    sin_t = sin_ref[...]
    halves = halves_ref[...]

    def lane_blocks(a):
        return [a[:, n * LANES:(n + 1) * LANES] for n in range(a.shape[1] // LANES)]

    def pair_norm(blks, g):
        sqs = [b * b for b in blks]
        his = [s.astype(BF16) for s in sqs]
        los = [(s - h_.astype(F32)).astype(BF16) for s, h_ in zip(sqs, his)]
        sums = [_dot(h_, halves) + _dot(l_, halves) for h_, l_ in zip(his, los)]
        return [b * lax.rsqrt(s / DA_HEAD_DIM + EPS) * g for b, s in zip(blks, sums)]

    def head_norm_rope(blks, g):
        sums = [jnp.sum(b * b * live, axis=-1, keepdims=True) for b in blks]
        normed = [b * lax.rsqrt(s / MLA_QK_DIM + EPS) * g for b, s in zip(blks, sums)]
        return [_rope128(y, cos_t, sin_t) for y in normed]

    def put_q(ref, vals):
        if prompt:
            outs = [jnp.transpose(v).astype(BF16) for v in vals]
            for n, o in enumerate(outs):
                ref[n * LANES:(n + 1) * LANES, :] = o
        else:
            for n, v in enumerate(vals):
                ref[:, n * LANES:(n + 1) * LANES] = v.astype(BF16)

    put_q(qd_ref, pair_norm(lane_blocks(_dot(hb, wm_ref[:, _Q0:_K0])), gq_ref[...]))
    kns = pair_norm(lane_blocks(_dot(hb, wm_ref[:, _K0:_V0])), gk_ref[...])
    va = _dot(hb, wm_ref[:, _V0:_CQ0])
    n_tok = x.shape[0]
    for hk, (kn, vh) in enumerate(zip(kns, lane_blocks(va))):
        kd_ref[pl.ds(hk, n_tok, stride=DA_KV_HEADS), :] = kn
        vd_ref[pl.ds(hk, n_tok, stride=DA_KV_HEADS), :] = vh
        if prompt:
            kdb_ref[:, hk * LANES:(hk + 1) * LANES] = kn.astype(BF16)
    if prompt:
        ones_rows = jnp.where(lax.broadcasted_iota(jnp.int32, (SUM_ROWS, x.shape[0]), 0) == 0, 1.0, 0.0).astype(BF16)
        for hk in range(DA_KV_HEADS):
            r0 = hk * (DA_V_DIM + SUM_ROWS)
            vdt_ref[r0:r0 + DA_V_DIM, :] = jnp.transpose(va[:, hk * LANES:(hk + 1) * LANES]).astype(BF16)
            vdt_ref[r0 + DA_V_DIM:r0 + DA_V_DIM + SUM_ROWS, :] = ones_rows

    cq = _dot(hb, wm_ref[:, _CQ0:_CKV0])
    cq = cq * lax.rsqrt(jnp.mean(cq * cq, axis=-1, keepdims=True) + EPS) * gcq_ref[...]
    put_q(qm_ref, head_norm_rope(lane_blocks(_dot(cq.astype(BF16), wuq_ref[...])), gmq_ref[...]))

    ckv = _dot(hb, wm_ref[:, _CKV0:_KR0])
    c = ckv * lax.rsqrt(jnp.mean(ckv * ckv, axis=-1, keepdims=True) + EPS) * gckv_ref[...]
    c_ref[...] = c
    krp = _dot(hb, wm_ref[:, _KR0:_MAIN_COLS])
    kr_ref[...] = krp[:, ROPE_LO:ROPE_LO + MLA_ROPE]

    if prompt:
        cb = c.astype(BF16)
        kn = _dot(cb, wuk_ref[...])
        kms = head_norm_rope([b + krp for b in lane_blocks(kn)], gmk_ref[...])
        for hm, km in enumerate(kms):
            km_ref[:, hm * LANES:(hm + 1) * LANES] = km.astype(BF16)
        vm = _dot(cb, wuv_ref[...])
        for blk in range(MLA_WIDTH // LANES):
            pair = jnp.transpose(vm[:, blk * LANES:(blk + 1) * LANES]).astype(BF16)
            for hh in range(2):
                r0 = (2 * blk + hh) * (MLA_V_DIM + SUM_ROWS)
                vmt_ref[r0:r0 + MLA_V_DIM, :] = pair[hh * MLA_V_DIM:(hh + 1) * MLA_V_DIM, :]
                vmt_ref[r0 + MLA_V_DIM:r0 + MLA_V_DIM + SUM_ROWS, :] = ones_rows


def _project(x2d, w, cos_tab, sin_tab, prompt, batch=None):
    n = x2d.shape[0]
    tm = min(TOKEN_TILE, n)
    assert n % tm == 0 and cos_tab.shape[0] % tm == 0
    n_pos = cos_tab.shape[0] // tm
    row = lambda cols: pl.BlockSpec((tm, cols), lambda i: (i, 0))
    pos = pl.BlockSpec((tm, LANES), lambda i: (i % n_pos, 0))
    consts = [w['norm_g'], w['w_main'], w['gq'], w['gk'], w['gcq'], w['w_uq'], w['gmq'], w['gckv'],
              w['w_uk_pad'], w['w_uv'], w['gmk'], w['halves']]
    sds = jax.ShapeDtypeStruct
    kv_shape = sds((n * DA_KV_HEADS, LANES), F32)
    kv_spec = pl.BlockSpec((tm * DA_KV_HEADS, LANES), lambda i: (i, 0))
    if prompt:
        s = cos_tab.shape[0]
        assert batch * s == n
        tr = lambda rows: pl.BlockSpec((None, rows, tm), lambda i: (i // n_pos, 0, i % n_pos))
        vdt_rows = DA_KV_HEADS * (DA_V_DIM + SUM_ROWS)
        vmt_rows = MLA_HEADS * (MLA_V_DIM + SUM_ROWS)
        out_shape = [sds((batch, DA_WIDTH, s), BF16), kv_shape, sds((n, 512), BF16), kv_shape,
                     sds((batch, vdt_rows, s), BF16), sds((n, MLA_KV_RANK), F32), sds((n, MLA_ROPE), F32),
                     sds((batch, MLA_HEADS * LANES, s), BF16), sds((n, MLA_HEADS * LANES), BF16),
                     sds((batch, vmt_rows, s), BF16)]
        out_specs = [tr(DA_WIDTH), kv_spec, row(512), kv_spec, tr(vdt_rows), row(MLA_KV_RANK), row(MLA_ROPE),
                     tr(MLA_HEADS * LANES), row(MLA_HEADS * LANES), tr(vmt_rows)]
    else:
        out_shape = [sds((n, DA_WIDTH), BF16), kv_shape, kv_shape, sds((n, MLA_KV_RANK), F32),
                     sds((n, MLA_ROPE), F32), sds((n, MLA_HEADS * LANES), BF16)]
        out_specs = [row(DA_WIDTH), kv_spec, kv_spec, row(MLA_KV_RANK), row(MLA_ROPE), row(MLA_HEADS * LANES)]
    return pl.pallas_call(
        functools.partial(_project_body, prompt),
        grid=(n // tm,),
        in_specs=[row(D_MODEL)] + [_const_spec(a.shape) for a in consts] + [pos, pos],
        out_specs=out_specs,
        out_shape=out_shape,
        compiler_params=pltpu.CompilerParams(dimension_semantics=("parallel",), vmem_limit_bytes=VMEM_LIMIT_BYTES),
        name="project_prompt" if prompt else "project_sample",
    )(x2d, *consts, cos_tab, sin_tab)


def _absorb_chunks(chunks, m_ref, acc_ref):
    stats = []
    for s, _, cols in chunks:
        m_old = m_ref[:, cols]
        m_new = jnp.maximum(m_old, jnp.max(s, axis=0, keepdims=True))
        m_ref[:, cols] = m_new
        stats.append((m_new, jnp.exp2(m_old - m_new)))
    probs = [jnp.exp2(s - m_new).astype(BF16) for (s, _, _), (m_new, _) in zip(chunks, stats)]
    for (_, vt, cols), (_, alpha), p in zip(chunks, stats, probs):
        acc_ref[:, cols] = alpha * acc_ref[:, cols] + _dot(vt, p)


def _key_tile_pipeline(i, qk, absorb, scores_ready=False, beside_last=None):
    if not scores_ready:
        qk(0, 0)

    def last(r, buf):
        if beside_last is not None:
            beside_last()
        absorb(r, buf)

    @pl.when(i == 0)
    def _():
        last(0, 0)

    @pl.when(i >= 1)
    def _():
        qk(1, 1)
        absorb(0, 0)

        @pl.when(i == 1)
        def _():
            last(1, 1)

        @pl.when(i >= 2)
        def _():
            qk(2, 0)
            absorb(1, 1)
            n_pairs = (i - 2) // 2

            def pair(p, carry):
                r = 2 + 2 * p
                qk(r + 1, 1)
                absorb(r, 0)
                qk(r + 2, 0)
                absorb(r + 1, 1)
                return carry
            lax.fori_loop(0, n_pairs, pair, 0)
            r = 2 + 2 * n_pairs

            @pl.when(r == i)
            def _():
                last(r, 0)

            @pl.when(r != i)
            def _():
                qk(r + 1, 1)
                absorb(r, 0)
                last(r + 1, 1)


def _diff_prompt_body(lam_ref, qt_ref, k_ref, vt_ref, bdiag_ref, bsub_ref, o_ref, *scratch):
    qs_refs, s_refs, m_refs, acc_refs = scratch[0:2], scratch[2:6], scratch[6:8], scratch[8:10]
    i = pl.program_id(2)
    t = ATT_TILE
    ch = Q_CHUNK
    per_g = t // ch
    n_chunks = DA_GROUP * per_g

    row = lax.broadcasted_iota(jnp.int32, (LANES, 1), 0)
    for g in range(DA_GROUP):
        qg = qt_ref[g * LANES:(g + 1) * LANES, :]
        qs_refs[0][:, g * t:(g + 1) * t] = jnp.where(row < DA_HEAD_DIM, qg, jnp.zeros_like(qg))
        qs_refs[1][:, g * t:(g + 1) * t] = jnp.where(row < DA_HEAD_DIM, jnp.zeros_like(qg), qg)
    for mp in range(2):
        m_refs[mp][...] = jnp.full(m_refs[mp].shape, MASKED, F32)
        acc_refs[mp][...] = jnp.zeros(acc_refs[mp].shape, F32)

    def n_keys(r, cc):
        return (cc + 1) * ch if isinstance(r, int) and r == 0 else t

    def stream(mp):
        def qk(r, buf):
            ks = pl.multiple_of((i - r) * t, t)
            for c in range(n_chunks):
                nk = n_keys(r, c % per_g)
                cols = slice(c * ch, (c + 1) * ch)
                s_refs[2 * mp + buf][0:nk, cols] = _dot(k_ref[pl.ds(ks, nk), :], qs_refs[mp][:, cols])

        def absorb(r, buf):
            ks = pl.multiple_of((i - r) * t, t)
            chunks = []
            for c in range(n_chunks):
                g, cc = divmod(c, per_g)
                nk = n_keys(r, cc)
                cols = slice(c * ch, (c + 1) * ch)
                s = s_refs[2 * mp + buf][0:nk, cols]
                if isinstance(r, int) and r == 0:
                    s = s + bdiag_ref[g, 0:nk, cc * ch:(cc + 1) * ch]
                elif isinstance(r, int) and r == 1 and cc == 0:
                    s = s + bsub_ref[g]
                chunks.append((s, vt_ref[:, pl.ds(ks, nk)], cols))
            _absorb_chunks(chunks, m_refs[mp], acc_refs[mp])
        return qk, absorb

    qk0, absorb0 = stream(0)
    qk1, absorb1 = stream(1)
    _key_tile_pipeline(i, qk0, absorb0, beside_last=lambda: qk1(0, 0))
    _key_tile_pipeline(i, qk1, absorb1, scores_ready=True)

    o0, o1 = [a[0:DA_V_DIM, :] * (1.0 / a[DA_V_DIM:DA_V_DIM + 1, :]) for a in acc_refs]
    comb = o0 - lam_ref[0] * o1
    for g in range(DA_GROUP):
        o_ref[:, g * LANES:(g + 1) * LANES] = jnp.transpose(comb[:, g * t:(g + 1) * t])


def _diff_prompt(lam, qdt, kdb, vdt, bias_diag, bias_sub):
    b, _, s = qdt.shape
    t = ATT_TILE
    assert s % t == 0
    vrows = DA_V_DIM + SUM_ROWS
    return pl.pallas_call(
        _diff_prompt_body,
        grid=(b, DA_KV_HEADS, s // t),
        in_specs=[pl.BlockSpec(memory_space=pltpu.SMEM),
                  pl.BlockSpec((None, DA_GROUP * LANES, t), lambda bi, h, i: (bi, h, i)),
                  pl.BlockSpec((None, s, LANES), lambda bi, h, i: (bi, 0, h)),
                  pl.BlockSpec((None, vrows, s), lambda bi, h, i: (bi, h, 0)),
                  pl.BlockSpec((DA_GROUP, t, t), lambda bi, h, i: (h, 0, 0)),
                  pl.BlockSpec((DA_GROUP, t, Q_CHUNK), lambda bi, h, i: (h, 0, 0))],
        out_specs=pl.BlockSpec((None, t, DA_GROUP * LANES), lambda bi, h, i: (bi, i, h)),
        out_shape=jax.ShapeDtypeStruct((b, s, DA_WIDTH), F32),
        scratch_shapes=([pltpu.VMEM((LANES, DA_GROUP * t), BF16)] * 2
                        + [pltpu.VMEM((t, DA_GROUP * t), F32)] * 4
                        + [pltpu.VMEM((1, DA_GROUP * t), F32)] * 2
                        + [pltpu.VMEM((vrows, DA_GROUP * t), F32)] * 2),
        compiler_params=pltpu.CompilerParams(
            dimension_semantics=("parallel", "parallel", "arbitrary"), vmem_limit_bytes=VMEM_LIMIT_BYTES),
        name="diff_prompt",
    )(lam, qdt, kdb, vdt, bias_diag, bias_sub)


def _mla_prompt_body(qt_ref, k_ref, vt_ref, o_ref, *scratch):
    s_refs, m_refs, acc_refs = scratch[0:4], scratch[4:6], scratch[6:8]
    i = pl.program_id(2)
    t = ATT_TILE
    ch = Q_CHUNK
    n_chunks = t // ch
    vrows = MLA_V_DIM + SUM_ROWS
    scale2 = (MLA_QK_DIM ** -0.5) * LOG2E
    for hh in range(2):
        m_refs[hh][...] = jnp.full(m_refs[hh].shape, MASKED, F32)
        acc_refs[hh][...] = jnp.zeros(acc_refs[hh].shape, F32)

    def n_keys(r, c):
        return (c + 1) * ch if isinstance(r, int) and r == 0 else t

    def stream(hh):
        hs = slice(hh * LANES, (hh + 1) * LANES)
        vs = slice(hh * vrows, (hh + 1) * vrows)

        def qk(r, buf):
            ks = pl.multiple_of((i - r) * t, t)
            for c in range(n_chunks):
                nk = n_keys(r, c)
                cols = slice(c * ch, (c + 1) * ch)
                s_refs[2 * hh + buf][0:nk, cols] = _dot(k_ref[pl.ds(ks, nk), hs], qt_ref[hs, cols])

        def absorb(r, buf):
            ks = pl.multiple_of((i - r) * t, t)
            chunks = []
            for c in range(n_chunks):
                nk = n_keys(r, c)
                cols = slice(c * ch, (c + 1) * ch)
                s = s_refs[2 * hh + buf][0:nk, cols] * scale2
                if isinstance(r, int) and r == 0:
                    kidx = lax.broadcasted_iota(jnp.int32, (nk, ch), 0)
                    qidx = lax.broadcasted_iota(jnp.int32, (nk, ch), 1) + c * ch
                    s = jnp.where(kidx <= qidx, s, MASKED)
                chunks.append((s, vt_ref[vs, pl.ds(ks, nk)], cols))
            _absorb_chunks(chunks, m_refs[hh], acc_refs[hh])
        return qk, absorb

    qk0, absorb0 = stream(0)
    qk1, absorb1 = stream(1)
    _key_tile_pipeline(i, qk0, absorb0, beside_last=lambda: qk1(0, 0))
    _key_tile_pipeline(i, qk1, absorb1, scores_ready=True)

    outs = [a[0:MLA_V_DIM, :] * (1.0 / a[MLA_V_DIM:MLA_V_DIM + 1, :]) for a in acc_refs]
    o_ref[...] = jnp.transpose(jnp.concatenate(outs, axis=0))


def _mla_prompt(qmt, km, vmt):
    b, _, s = qmt.shape
    t = ATT_TILE
    vrows = MLA_V_DIM + SUM_ROWS
    return pl.pallas_call(
        _mla_prompt_body,
        grid=(b, MLA_HEADS // 2, s // t),
        in_specs=[pl.BlockSpec((None, 2 * LANES, t), lambda bi, hp, i: (bi, hp, i)),
                  pl.BlockSpec((None, s, 2 * LANES), lambda bi, hp, i: (bi, 0, hp)),
                  pl.BlockSpec((None, 2 * vrows, s), lambda bi, hp, i: (bi, hp, 0))],
        out_specs=pl.BlockSpec((None, t, 2 * MLA_V_DIM), lambda bi, hp, i: (bi, i, hp)),
        out_shape=jax.ShapeDtypeStruct((b, s, MLA_WIDTH), F32),
        scratch_shapes=([pltpu.VMEM((t, t), F32)] * 4
                        + [pltpu.VMEM((1, t), F32)] * 2
                        + [pltpu.VMEM((vrows, t), F32)] * 2),
        compiler_params=pltpu.CompilerParams(
            dimension_semantics=("parallel", "parallel", "arbitrary"), vmem_limit_bytes=VMEM_LIMIT_BYTES),
        name="mla_prompt",
    )(qmt, km, vmt)


def _sample_body(n_steps, pt_ref, lam_ref, ck_hbm, cv_hbm, cc_hbm, cr_hbm,
                 kn_ref, vn_ref, cn_ref, rn_ref, qbd_ref, qmn_ref, qrr_ref, gkn_ref, wuk_ref, wuv_ref, eseg_ref,
                 bias_ref, mmask_ref, ctab_ref, stab_ref,
                 oa_ref, ob_ref,
                 qabs_ref, md_ref, ld_ref, accd_ref, mm_ref, lm_ref, accm_ref, kbuf, vbuf, cbuf, rbuf, sems):
    pg = PAGES_PER_STEP
    b = pl.program_id(0)
    j = pl.program_id(1)
    scale_m = (MLA_QK_DIM ** -0.5) * LOG2E
    page = cc_hbm.shape[1]
    krows = page * DA_KV_HEADS
    t_new = cn_ref.shape[0]
    n_mla = MLA_HEADS * t_new

    def page_copies(pid, p, slot):
        return (pltpu.make_async_copy(ck_hbm.at[pid], kbuf.at[slot, pl.ds(p * krows, krows), :], sems.at[slot, 0]),
                pltpu.make_async_copy(cv_hbm.at[pid], vbuf.at[slot, pl.ds(p * krows, krows), :], sems.at[slot, 1]),
                pltpu.make_async_copy(cc_hbm.at[pid], cbuf.at[slot, pl.ds(p * page, page), :], sems.at[slot, 2]),
                pltpu.make_async_copy(cr_hbm.at[pid], rbuf.at[slot, :, pl.ds(p * page, page)], sems.at[slot, 3]))

    def start_step(bb, jj, slot):
        for p in range(pg):
            for cp in page_copies(pt_ref[bb, jj * pg + p], p, slot):
                cp.start()

    def wait_step(slot):
        for p in range(pg):
            for cp in page_copies(0, p, slot):
                cp.wait()

    g = b * n_steps + j
    n_global = pt_ref.shape[0] * n_steps
    slot = lax.rem(g, PAGE_SLOTS)
    ahead = PAGE_SLOTS - 1

    @pl.when(g == 0)
    def _():
        for s0 in range(min(ahead, n_global)):
            start_step(s0 // n_steps, s0 % n_steps, s0 % PAGE_SLOTS)

    @pl.when(g + ahead < n_global)
    def _():
        nxt = g + ahead
        start_step(nxt // n_steps, lax.rem(nxt, n_steps), lax.rem(nxt, PAGE_SLOTS))

    wait_step(slot)

    def heads_to_lanes(ref2, start=0, t=None):
        t = ref2.shape[0] // DA_KV_HEADS if t is None else t
        return jnp.concatenate([ref2[pl.ds(start + h, t, stride=DA_KV_HEADS), :].astype(BF16)
                                for h in range(DA_KV_HEADS)], axis=1)

    def rows_from_lanes(alpha, n_rows):
        full = jnp.broadcast_to(alpha, (LANES, alpha.shape[1]))
        if alpha.shape[1] < LANES:
            full = jnp.concatenate([full, jnp.zeros((LANES, LANES - alpha.shape[1]), F32)], axis=1)
        return jnp.transpose(full)[:n_rows, :]

    def score_matmuls(kb, c, krt, ctab, stab):
        cb = c.astype(BF16)
        s = _dot(kb, qbd_ref[...])
        kn = _dot(cb, wuk_ref[...])
        raw = _dot(cb, qabs_ref[...])
        ext = _dot_tn(jnp.concatenate([krt * ctab, krt * stab, krt * krt], axis=0).astype(BF16), qrr_ref[...])
        return cb, s, kn, raw, ext

    def local_softmaxes(blocks):
        half = MLA_HEADS * NOPE_HALF
        ss = []
        for (cb, s, kn, raw, ext), _, _, _ in blocks:
            kn2 = kn * kn
            ss.append(_dot((kn2[:, :half] + kn2[:, half:]).astype(BF16), eseg_ref[...]))
        sa = [s + bias for (_, s, _, _, _), _, bias, _ in blocks]
        ma = [jnp.max(s, axis=0, keepdims=True) for s in sa]
        pa = [jnp.exp2(s - m) for s, m in zip(sa, ma)]
        sb = []
        for ((_, _, _, raw, ext), _, _, mmask), q in zip(blocks, ss):
            r = lax.rsqrt((q + pltpu.roll(ext, LANES - n_mla, 1)[:, :n_mla]) / MLA_QK_DIM + EPS)
            sm = (raw + ext[:, :n_mla]) * r * scale_m
            sb.append(sm if mmask is None else sm + mmask)
        mb = [jnp.max(s, axis=0, keepdims=True) for s in sb]
        pb_ = [jnp.exp2(s - m) for s, m in zip(sb, mb)]
        parts_a = [(m, jnp.sum(p, axis=0, keepdims=True), _dot_tn(p.astype(BF16), vb))
                   for m, p, (_, vb, _, _) in zip(ma, pa, blocks)]
        parts_b = [(m, jnp.sum(p, axis=0, keepdims=True), _dot_tn(p.astype(BF16), pre[0]))
                   for m, p, (pre, _, _, _) in zip(mb, pb_, blocks)]
        return parts_a, parts_b

    def fold(m_ref, l_ref, acc_ref, parts):
        n_rows, width = acc_ref.shape
        m_old = m_ref[...]
        m_new = m_old
        for m, _, _ in parts:
            m_new = jnp.maximum(m_new, m)
        wide = lambda f: jnp.concatenate([rows_from_lanes(f, n_rows)] * (width // LANES), axis=1)
        alpha = jnp.exp2(m_old - m_new)
        l_new = alpha * l_ref[...]
        acc = wide(alpha) * acc_ref[...]
        for m, l, a in parts:
            f = jnp.exp2(m - m_new)
            l_new = l_new + f * l
            acc = acc + wide(f) * a
        m_ref[...] = m_new
        l_ref[...] = l_new
        acc_ref[...] = acc

    @pl.when(j == 0)
    def _():
        qg = (qmn_ref[...].astype(F32) * gkn_ref[...]).astype(BF16)
        qabs_ref[...] = _dot(wuk_ref[...], qg).astype(BF16)
        past = n_steps * pg * page
        rows_n = pl.ds(past, t_new)
        pre = score_matmuls(heads_to_lanes(kn_ref), cn_ref[...], rn_ref[...], ctab_ref[:, rows_n], stab_ref[:, rows_n])
        ((m_d, l_d, pv),), ((m_m, l_m, pc),) = local_softmaxes(
            [(pre, heads_to_lanes(vn_ref), bias_ref[rows_n, :], mmask_ref[...])])
        md_ref[...] = m_d
        ld_ref[...] = l_d
        accd_ref[...] = pv
        mm_ref[...] = m_m
        lm_ref[...] = l_m
        accm_ref[...] = pc

    base = j * (pg * page)
    pb = PAGES_PER_BLOCK
    blocks = []
    tb = pb * page
    for blk in range(pg // pb):
        rows = pl.ds(pl.multiple_of(base + blk * tb, tb), tb)
        keys = slice(blk * tb, (blk + 1) * tb)
        pre = score_matmuls(heads_to_lanes(kbuf.at[slot], blk * tb * DA_KV_HEADS, tb), cbuf[slot, keys, :],
                            rbuf[slot, :, keys], ctab_ref[:, rows], stab_ref[:, rows])
        blocks.append((pre, heads_to_lanes(vbuf.at[slot], blk * tb * DA_KV_HEADS, tb), bias_ref[rows, :], None))
    parts_d, parts_m = local_softmaxes(blocks)
    fold(md_ref, ld_ref, accd_ref, parts_d)
    fold(mm_ref, lm_ref, accm_ref, parts_m)

    @pl.when(j == n_steps - 1)
    def _():
        lam = lam_ref[0]
        inv_d = rows_from_lanes(1.0 / ld_ref[...], LANES)
        accd = accd_ref[...]
        lanes_per_kv = 2 * DA_GROUP * t_new
        for h in range(DA_KV_HEADS):
            cols = slice(h * LANES, (h + 1) * LANES)
            for g in range(DA_GROUP):
                r1 = h * lanes_per_kv + g * t_new
                r2 = r1 + DA_GROUP * t_new
                o1 = accd[r1:r1 + t_new, cols] * inv_d[r1:r1 + t_new, :]
                o2 = accd[r2:r2 + t_new, cols] * inv_d[r2:r2 + t_new, :]
                hq = h * DA_GROUP + g
                oa_ref[:, hq * LANES:(hq + 1) * LANES] = o1 - lam * o2
        inv_m = rows_from_lanes(1.0 / lm_ref[...], n_mla)
        cbar = (accm_ref[...] * jnp.concatenate([inv_m] * (MLA_KV_RANK // LANES), axis=1)).astype(BF16)
        full = _dot(cbar, wuv_ref[...])
        lane5 = lax.broadcasted_iota(jnp.int32, (1, MLA_WIDTH), 1)
        ob = jnp.zeros((t_new, MLA_WIDTH), F32)
        for h in range(MLA_HEADS):
            sel = (lane5 >= h * MLA_V_DIM) & (lane5 < (h + 1) * MLA_V_DIM)
            ob = ob + jnp.where(sel, full[h * t_new:(h + 1) * t_new, :], 0.0)
        ob_ref[...] = ob


def _sample_attention(lam, page_table, ck, cv, cc, crt, new, qbd, qmn, qrr, w, bias_tab, mla_mask, ctab, stab):
    n_seq, n_pages = page_table.shape
    pool, page = cc.shape[0], cc.shape[1]
    t_new = new['c'].shape[1]
    pg = PAGES_PER_STEP
    assert n_pages % pg == 0 and pg % PAGES_PER_BLOCK == 0 and MLA_HEADS * t_new <= LANES // 2
    n_steps = n_pages // pg

    def seq_spec(*tail):
        zeros = (0,) * len(tail)
        return pl.BlockSpec((None,) + tail, lambda b, j, pt: (b,) + zeros)

    def const(shape):
        nd = len(shape)
        return pl.BlockSpec(shape, lambda b, j, pt: (0,) * nd, pipeline_mode=pl.Buffered(1))

    in_specs = [pl.BlockSpec(memory_space=pltpu.SMEM)] + [pl.BlockSpec(memory_space=pl.ANY)] * 4
    per_seq = [new['kd'], new['vd'], new['c'], new['krt'], qbd, qmn, qrr]
    in_specs += [seq_spec(*a.shape[1:]) for a in per_seq]
    consts = [w['gk_nope_col'], w['w_uk'], w['w_uv'], w['eseg'], bias_tab, mla_mask, ctab, stab]
    in_specs += [const(a.shape) for a in consts]
    operands = [ck, cv, cc, crt] + per_seq + consts
    krows = page * DA_KV_HEADS
    grid_spec = pltpu.PrefetchScalarGridSpec(
        num_scalar_prefetch=1,
        grid=(n_seq, n_steps),
        in_specs=in_specs,
        out_specs=[pl.BlockSpec((None, t_new, DA_WIDTH), lambda b, j, pt: (b, 0, 0)),
                   pl.BlockSpec((None, t_new, MLA_WIDTH), lambda b, j, pt: (b, 0, 0))],
        scratch_shapes=[pltpu.VMEM((MLA_KV_RANK, MLA_HEADS * t_new), BF16),
                        pltpu.VMEM((1, LANES), F32), pltpu.VMEM((1, LANES), F32),
                        pltpu.VMEM((LANES, DA_KV_HEADS * DA_V_DIM), F32),
                        pltpu.VMEM((1, MLA_HEADS * t_new), F32), pltpu.VMEM((1, MLA_HEADS * t_new), F32),
                        pltpu.VMEM((MLA_HEADS * t_new, MLA_KV_RANK), F32),
                        pltpu.VMEM((PAGE_SLOTS, pg * krows, LANES), F32),
                        pltpu.VMEM((PAGE_SLOTS, pg * krows, DA_V_DIM), F32),
                        pltpu.VMEM((PAGE_SLOTS, pg * page, MLA_KV_RANK), F32),
                        pltpu.VMEM((PAGE_SLOTS, MLA_ROPE, pg * page), F32),
                        pltpu.SemaphoreType.DMA((PAGE_SLOTS, 4))])
    return pl.pallas_call(
        functools.partial(_sample_body, n_steps),
        grid_spec=grid_spec,
        out_shape=[jax.ShapeDtypeStruct((n_seq, t_new, DA_WIDTH), F32),
                   jax.ShapeDtypeStruct((n_seq, t_new, MLA_WIDTH), F32)],
        compiler_params=pltpu.CompilerParams(dimension_semantics=("arbitrary", "arbitrary"),
                                             vmem_limit_bytes=VMEM_LIMIT_BYTES),
        name="sample_attention",
    )(page_table, lam, *operands)


def _merge_body(one_minus_lam_init, x_ref, oa_ref, ob_ref, ng_ref, wzg_ref, gout_ref, wpa_ref, wpb_ref, wo_ref, y_ref):
    x = x_ref[...]
    h = x * lax.rsqrt(jnp.mean(x * x, axis=-1, keepdims=True) + EPS) * ng_ref[...]
    hb = h.astype(BF16)
    za = _dot(hb, wzg_ref[:, 0:DA_WIDTH])
    gout = gout_ref[...]
    parts = []
    for hq in range(DA_HEADS):
        sl = slice(hq * LANES, (hq + 1) * LANES)
        o = oa_ref[:, sl]
        o = o * lax.rsqrt(jnp.mean(o * o, axis=-1, keepdims=True) + EPS) * gout * one_minus_lam_init
        z = za[:, sl]
        parts.append((o * (z * jax.nn.sigmoid(z))).astype(BF16))
    ya = _dot(jnp.concatenate(parts, axis=1), wpa_ref[...])
    zb = _dot(hb, wzg_ref[:, DA_WIDTH:DA_WIDTH + MLA_WIDTH])
    yb = _dot((ob_ref[...] * (zb * jax.nn.sigmoid(zb))).astype(BF16), wpb_ref[...])
    g0 = DA_WIDTH + MLA_WIDTH
    ga = _dot(hb, wzg_ref[:, g0:g0 + D_MODEL])
    gb = _dot(hb, wzg_ref[:, g0 + D_MODEL:g0 + 2 * D_MODEL])
    mix = jax.nn.sigmoid(ga) * ya + jax.nn.sigmoid(gb) * yb
    y_ref[...] = x + _dot(mix.astype(BF16), wo_ref[...])


def _merge(x2d, oa, ob, w, lam_init):
    n = x2d.shape[0]
    tm = min(TOKEN_TILE, n)
    assert n % tm == 0
    row = lambda cols: pl.BlockSpec((tm, cols), lambda i: (i, 0))
    consts = [w['norm_g'], w['w_zg'], w['gout'], w['w_pa'], w['w_pb'], w['w_o']]
    return pl.pallas_call(
        functools.partial(_merge_body, 1.0 - lam_init),
        grid=(n // tm,),
        in_specs=[row(D_MODEL), row(DA_WIDTH), row(MLA_WIDTH)] + [_const_spec(a.shape) for a in consts],
        out_specs=row(D_MODEL),
        out_shape=jax.ShapeDtypeStruct((n, D_MODEL), F32),
        compiler_params=pltpu.CompilerParams(dimension_semantics=("parallel",), vmem_limit_bytes=VMEM_LIMIT_BYTES),
        name="merge",
    )(x2d, oa, ob, *consts)


def _rel_bucket(dist):
    n = jnp.maximum(dist, 0)
    max_exact = REL_BUCKETS // 2
    nf = jnp.maximum(n, 1).astype(F32)
    large = max_exact + (jnp.log(nf / max_exact) / math.log(REL_MAX_DIST / max_exact)
                         * (REL_BUCKETS - max_exact)).astype(jnp.int32)
    large = jnp.minimum(large, REL_BUCKETS - 1)
    return jnp.where(n < max_exact, n, large)


def _rope_angles(pos):
    inv = jnp.power(ROPE_THETA, -jnp.arange(HALF, dtype=F32) / HALF)
    ang = pos.astype(F32)[:, None] * inv[None, :]
    return jnp.cos(ang), jnp.sin(ang)


def _rope_tables128(pos):
    cos, sin = _rope_angles(pos)
    n = pos.shape[0]
    cos_t = jnp.concatenate([jnp.ones((n, MLA_NOPE), F32), cos, cos, jnp.zeros((n, LANES - MLA_QK_DIM), F32)], axis=1)
    sin_t = jnp.concatenate([jnp.zeros((n, MLA_NOPE), F32), -sin, sin, jnp.zeros((n, LANES - MLA_QK_DIM), F32)], axis=1)
    return cos_t, sin_t


def _pad_heads(a, live):
    pad = [(0, 0)] * (a.ndim - 1) + [(0, LANES - live)]
    a = jnp.pad(a, pad)
    return a.reshape(a.shape[:-2] + (a.shape[-2] * LANES,))


def _prep_layer(p, t_new):
    offs = [int(v) for v in np.cumsum((0,) + IN_SPLITS)]
    wq, wk, wv, wza, wcq, wckv, wkr, wzb, wga, wgb = [p['w_in'][:, offs[n]:offs[n + 1]] for n in range(10)]
    with_x1_copy = lambda a: jnp.concatenate([a, a[..., MLA_NOPE:MLA_NOPE + HALF]], axis=-1)
    tail = LANES - MLA_QK_DIM - HALF
    wkr_placed = jnp.pad(jnp.concatenate([wkr, wkr[:, :HALF]], axis=1), ((0, 0), (ROPE_LO, tail)))
    halves = np.arange(LANES) // DA_HEAD_DIM
    w_ukv = p['w_ukv'].reshape(MLA_KV_RANK, MLA_HEADS, MLA_NOPE + MLA_V_DIM)
    w_uk = w_ukv[:, :, :MLA_NOPE]
    w_uk_halves = jnp.transpose(w_uk.reshape(MLA_KV_RANK, MLA_HEADS, 2, NOPE_HALF), (0, 2, 1, 3))
    gmk = p['mla_k_norm']
    gk_nope = jnp.broadcast_to(gmk[:MLA_NOPE].reshape(2, 1, NOPE_HALF), (2, MLA_HEADS, NOPE_HALF))
    head_of_row = np.arange(MLA_HEADS * NOPE_HALF) // NOPE_HALF
    head_of_lane = np.arange(MLA_HEADS * t_new) // t_new
    return dict(
        norm_g=p['norm_g'].reshape(1, D_MODEL),
        w_main=jnp.concatenate([wq, wk, wv, wcq, wckv, wkr_placed], axis=1).astype(BF16),
        w_zg=jnp.concatenate([wza, wzb, wga, wgb], axis=1).astype(BF16),
        gq=p['da_q_norm'].reshape(1, LANES) * ((DA_HEAD_DIM ** -0.5) * LOG2E),
        gk=p['da_k_norm'].reshape(1, LANES),
        gcq=p['mla_cq_norm'].reshape(1, MLA_Q_RANK),
        w_uq=_pad_heads(with_x1_copy(p['w_uq'].reshape(MLA_Q_RANK, MLA_HEADS, MLA_QK_DIM)),
                        MLA_QK_DIM + HALF).astype(BF16),
        gmq=jnp.pad(with_x1_copy(p['mla_q_norm']), (0, tail)).reshape(1, LANES),
        halves=jnp.asarray(halves[:, None] == halves[None, :], BF16),
        gckv=p['mla_ckv_norm'].reshape(1, MLA_KV_RANK),
        w_uk_pad=_pad_heads(w_uk, MLA_NOPE).astype(BF16),
        w_uk=w_uk_halves.reshape(MLA_KV_RANK, MLA_HEADS * MLA_NOPE).astype(BF16),
        w_uv=w_ukv[:, :, MLA_NOPE:].reshape(MLA_KV_RANK, MLA_WIDTH).astype(BF16),
        gmk=jnp.pad(with_x1_copy(gmk), (0, tail)).reshape(1, LANES),
        gk_nope_col=gk_nope.reshape(MLA_HEADS * MLA_NOPE, 1),
        gk_rope=gmk[MLA_NOPE:],
        eseg=jnp.asarray(head_of_row[:, None] == head_of_lane[None, :], BF16),
        gout=p['da_out_norm'].reshape(1, DA_V_DIM),
        w_pa=p['w_pa'].astype(BF16), w_pb=p['w_pb'].astype(BF16), w_o=p['w_o'].astype(BF16),
    )


def _toeplitz(v, n):
    lead = v.shape[:-1]
    u = jnp.flip(v, axis=-1)
    w = jnp.concatenate([u, jnp.zeros(lead + (1,), v.dtype)], axis=-1)
    flat = jnp.tile(w, (1,) * len(lead) + (n,))[..., :n * (2 * n - 1)]
    return flat.reshape(lead + (n, 2 * n - 1))[..., n - 1:]


def _prompt_bias_tiles(rel_bias):
    t = ATT_TILE
    far = rel_bias[REL_BUCKETS - 1]
    d = jnp.arange(2 * t - 1, dtype=jnp.int32) - (t - 1)
    f = jnp.where((d >= 0)[:, None], rel_bias[_rel_bucket(d)] - far[None, :], MASKED)
    diag = _toeplitz(f.T, t)
    dc = LANES + jnp.arange(2 * LANES - 1, dtype=jnp.int32) - (LANES - 1)
    corner = _toeplitz((rel_bias[_rel_bucket(dc)] - far[None, :]).T, LANES)
    sub = jnp.pad(jnp.swapaxes(corner, 1, 2), ((0, 0), (t - LANES, 0), (0, Q_CHUNK - LANES)))
    return jnp.swapaxes(diag, 1, 2) * LOG2E, sub * LOG2E


def _sample_tables(rel_bias, gk_rope, past, t_new):
    n_keys = past + t_new
    near = REL_MAX_DIST + t_new
    k_pos = jnp.arange(n_keys - near, n_keys, dtype=jnp.int32)
    q_pos = past + jnp.arange(t_new, dtype=jnp.int32)
    b = rel_bias[_rel_bucket(q_pos[None, :] - k_pos[:, None])]
    b = b - rel_bias[REL_BUCKETS - 1][None, None, :]
    b = jnp.where((q_pos[None, :] >= k_pos[:, None])[:, :, None], b, MASKED)
    b = jnp.transpose(b.reshape(near, t_new, DA_KV_HEADS, DA_GROUP), (0, 2, 3, 1))
    b = jnp.broadcast_to(b[:, :, None], (near, DA_KV_HEADS, 2, DA_GROUP, t_new)).reshape(near, 2 * DA_HEADS * t_new)
    b = jnp.pad(b.astype(F32) * LOG2E, ((n_keys - near, 0), (0, 0)))
    newk = jnp.arange(t_new)
    mm = jnp.where(newk[None, :] >= newk[:, None], 0.0, MASKED).astype(F32)
    mm = jnp.tile(mm, (1, MLA_HEADS))
    cos, sin = _rope_angles(jnp.arange(n_keys, dtype=jnp.int32))
    ctab = jnp.concatenate([cos, cos], axis=1) * gk_rope[None, :]
    stab = jnp.concatenate([sin, sin], axis=1) * gk_rope[None, :]
    return b, mm, ctab.T, stab.T


def _sample_query_layouts(qd, qm, n_seq, t_new):
    q = qd.reshape(n_seq, t_new, DA_KV_HEADS, DA_GROUP, 2, DA_HEAD_DIM)
    q = jnp.transpose(q, (0, 2, 4, 5, 3, 1)).reshape(n_seq, DA_KV_HEADS * 2, DA_HEAD_DIM, DA_GROUP * t_new)
    eye = jnp.eye(DA_KV_HEADS * 2, dtype=qd.dtype)
    qbd = q[:, :, :, None, :] * eye[None, :, None, :, None]
    qbd = qbd.reshape(n_seq, DA_KV_HEADS * 2 * DA_HEAD_DIM, DA_KV_HEADS * 2 * DA_GROUP * t_new)
    qh = qm.reshape(n_seq, t_new, MLA_HEADS, LANES)
    qn = jnp.transpose(qh[..., :MLA_NOPE].reshape(n_seq, t_new, MLA_HEADS, 2, NOPE_HALF), (0, 3, 2, 4, 1))
    eye_h = jnp.eye(MLA_HEADS, dtype=qm.dtype)
    qmn = (qn[:, :, :, :, None, :] * eye_h[None, None, :, None, :, None]).reshape(
        n_seq, MLA_HEADS * MLA_NOPE, MLA_HEADS * t_new)
    qr = jnp.transpose(qh[..., ROPE_LO:ROPE_LO + MLA_ROPE], (0, 3, 2, 1)).reshape(n_seq, MLA_ROPE, MLA_HEADS * t_new)
    qrs = jnp.concatenate([qr[:, HALF:], -qr[:, :HALF]], axis=1)
    n_mla = MLA_HEADS * t_new
    top = jnp.pad(jnp.concatenate([qr, qrs], axis=1), ((0, 0), (0, 0), (0, LANES - n_mla)))
    bottom = jnp.pad(jnp.ones((n_seq, MLA_ROPE, n_mla), qm.dtype), ((0, 0), (0, 0), (LANES - n_mla, 0)))
    return qbd, qmn, jnp.concatenate([top, bottom], axis=1)


def kernel(x_prompt, x_sample, cache_diff_k, cache_diff_v, cache_mla_latent, cache_mla_krope, page_table, rel_bias,
           norm_g, w_in, da_q_norm, da_k_norm, lam_q1, lam_k1, lam_q2, lam_k2, da_out_norm, w_pa, mla_cq_norm, w_uq,
           mla_ckv_norm, w_ukv, mla_q_norm, mla_k_norm, w_pb, w_o):
    b, s, _ = x_prompt.shape
    n_seq, t_new, _ = x_sample.shape
    depth, pool, page = cache_diff_k.shape[:3]
    past = page_table.shape[1] * page
    page_table = page_table.astype(jnp.int32)

    cos_p, sin_p = _rope_tables128(jnp.arange(s, dtype=jnp.int32))
    rows_s = min(TOKEN_TILE, n_seq * t_new)
    cos_s, sin_s = _rope_tables128(past + (jnp.arange(rows_s, dtype=jnp.int32) % t_new))
    bias_diag, bias_sub = _prompt_bias_tiles(rel_bias)

    xp = x_prompt.reshape(b * s, D_MODEL)
    xs = x_sample.reshape(n_seq * t_new, D_MODEL)
    news = [[] for _ in range(8)]
    for l in range(depth):
        p = dict(norm_g=norm_g[l], w_in=w_in[l], da_q_norm=da_q_norm[l], da_k_norm=da_k_norm[l],
                 da_out_norm=da_out_norm[l], w_pa=w_pa[l], mla_cq_norm=mla_cq_norm[l], w_uq=w_uq[l],
                 mla_ckv_norm=mla_ckv_norm[l], w_ukv=w_ukv[l], mla_q_norm=mla_q_norm[l],
                 mla_k_norm=mla_k_norm[l], w_pb=w_pb[l], w_o=w_o[l])
        w = _prep_layer(p, t_new)
        lam_init = 0.8 - 0.6 * math.exp(-0.3 * l)
        lam = (jnp.exp(jnp.sum(lam_q1[l] * lam_k1[l]).astype(F32))
               - jnp.exp(jnp.sum(lam_q2[l] * lam_k2[l]).astype(F32)) + lam_init).reshape(1)

        qdt, kd, kdb, vd, vdt, c, kr, qmt, km, vmt = _project(xp, w, cos_p, sin_p, True, batch=b)
        oa_p = _diff_prompt(lam, qdt, kdb.reshape(b, s, 512), vdt, bias_diag, bias_sub)
        ob_p = _mla_prompt(qmt, km.reshape(b, s, MLA_HEADS * LANES), vmt)

        qd_s, kd_s, vd_s, c_s, kr_s, qm_s = _project(xs, w, cos_s, sin_s, False)
        qbd, qmn, qrr = _sample_query_layouts(qd_s, qm_s, n_seq, t_new)
        bias_tab, mla_mask, ctab, stab = _sample_tables(rel_bias, w['gk_rope'], past, t_new)
        new = dict(kd=kd_s.reshape(n_seq, t_new * DA_KV_HEADS, LANES), vd=vd_s.reshape(n_seq, t_new * DA_KV_HEADS, DA_V_DIM),
                   c=c_s.reshape(n_seq, t_new, MLA_KV_RANK),
                   krt=jnp.swapaxes(kr_s.reshape(n_seq, t_new, MLA_ROPE), 1, 2))
        oa_s, ob_s = _sample_attention(
            lam, page_table, cache_diff_k[l].reshape(pool, page * DA_KV_HEADS, LANES),
            cache_diff_v[l].reshape(pool, page * DA_KV_HEADS, DA_V_DIM), cache_mla_latent[l],
            jnp.swapaxes(cache_mla_krope[l], 1, 2), new, qbd, qmn, qrr, w, bias_tab, mla_mask, ctab, stab)

        xp = _merge(xp, oa_p.reshape(b * s, DA_WIDTH), ob_p.reshape(b * s, MLA_WIDTH), w, lam_init)
        xs = _merge(xs, oa_s.reshape(n_seq * t_new, DA_WIDTH), ob_s.reshape(n_seq * t_new, MLA_WIDTH), w, lam_init)

        for lst, a in zip(news, (kd.reshape(b, s, DA_KV_HEADS, LANES), vd.reshape(b, s, DA_KV_HEADS, DA_V_DIM),
                                 c.reshape(b, s, MLA_KV_RANK), kr.reshape(b, s, MLA_ROPE),
                                 kd_s.reshape(n_seq, t_new, DA_KV_HEADS, LANES),
                                 vd_s.reshape(n_seq, t_new, DA_KV_HEADS, DA_V_DIM),
                                 c_s.reshape(n_seq, t_new, MLA_KV_RANK), kr_s.reshape(n_seq, t_new, MLA_ROPE))):
            lst.append(a)
    return (xp.reshape(b, s, D_MODEL), xs.reshape(n_seq, t_new, D_MODEL)) + tuple(jnp.stack(a) for a in news)
```

```python
import functools
import math

import numpy as np
import jax
import jax.numpy as jnp
from jax import lax
from jax.experimental import pallas as pl
from jax.experimental.pallas import tpu as pltpu

F32 = jnp.float32
BF16 = jnp.bfloat16

D_MODEL = 1024
DA_HEADS = 8
DA_KV_HEADS = 4
DA_GROUP = DA_HEADS // DA_KV_HEADS
DA_HEAD_DIM = 64
DA_V_DIM = 2 * DA_HEAD_DIM
DA_WIDTH = DA_HEADS * DA_V_DIM
MLA_HEADS = 8
MLA_Q_RANK = 384
MLA_KV_RANK = 256
MLA_NOPE = 64
MLA_ROPE = 32
MLA_QK_DIM = MLA_NOPE + MLA_ROPE
MLA_V_DIM = 64
MLA_WIDTH = MLA_HEADS * MLA_V_DIM
ROPE_THETA = 10000.0
REL_BUCKETS = 32
REL_MAX_DIST = 128
EPS = 1e-6
IN_SPLITS = (DA_HEADS * 2 * DA_HEAD_DIM, DA_KV_HEADS * 2 * DA_HEAD_DIM, DA_KV_HEADS * DA_V_DIM, DA_WIDTH,
             MLA_Q_RANK, MLA_KV_RANK, MLA_ROPE, MLA_WIDTH, D_MODEL, D_MODEL)

LANES = 128
VMEM_LIMIT_BYTES = 56 * 1024 * 1024

MASKED = -1e30
HALF = MLA_ROPE // 2
ROPE_LO = MLA_NOPE
NOPE_HALF = MLA_NOPE // 2
SUM_ROWS = 16
LOG2E = math.log2(math.e)

_Q0, _K0, _V0, _CQ0, _CKV0, _KR0, _MAIN_COLS = 0, 1024, 1536, 2048, 2432, 2688, 2816

TOKEN_TILE = 512
ATT_TILE = 512
Q_CHUNK = 256
PAGES_PER_STEP = 8
PAGES_PER_BLOCK = 2
PAGE_SLOTS = 3


def _const_spec(shape):
    nd = len(shape)
    return pl.BlockSpec(shape, lambda *_: (0,) * nd, pipeline_mode=pl.Buffered(1))


def _dot(a, b):
    return jnp.dot(a, b, preferred_element_type=F32)


def _dot_tn(a, b):
    return lax.dot_general(a, b, (((0,), (0,)), ((), ())), preferred_element_type=F32)


def _lane_iota():
    return lax.broadcasted_iota(jnp.int32, (1, LANES), 1)


def _rope128(y, cos_t, sin_t):
    return y * cos_t + pltpu.roll(y, LANES - HALF, 1) * sin_t


def _project_body(prompt, x_ref, ng_ref, wm_ref, gq_ref, gk_ref, gcq_ref, wuq_ref, gmq_ref, gckv_ref,
                  wuk_ref, wuv_ref, gmk_ref, halves_ref, cos_ref, sin_ref, *out_refs):
    if prompt:
        qd_ref, kd_ref, kdb_ref, vd_ref, vdt_ref, c_ref, kr_ref, qm_ref, km_ref, vmt_ref = out_refs
    else:
        qd_ref, kd_ref, vd_ref, c_ref, kr_ref, qm_ref = out_refs
    x = x_ref[...]
    h = x * lax.rsqrt(jnp.mean(x * x, axis=-1, keepdims=True) + EPS) * ng_ref[...]
    hb = h.astype(BF16)
    live = (_lane_iota() < MLA_QK_DIM).astype(F32)
    cos_t = cos_ref[...]
    sin_t = sin_ref[...]
    halves = halves_ref[...]

    def lane_blocks(a):
        return [a[:, n * LANES:(n + 1) * LANES] for n in range(a.shape[1] // LANES)]

    def pair_norm(blks, g):
        sqs = [b * b for b in blks]
        his = [s.astype(BF16) for s in sqs]
        los = [(s - h_.astype(F32)).astype(BF16) for s, h_ in zip(sqs, his)]
        sums = [_dot(h_, halves) + _dot(l_, halves) for h_, l_ in zip(his, los)]
        return [b * lax.rsqrt(s / DA_HEAD_DIM + EPS) * g for b, s in zip(blks, sums)]

    def head_norm_rope(blks, g):
        sums = [jnp.sum(b * b * live, axis=-1, keepdims=True) for b in blks]
        normed = [b * lax.rsqrt(s / MLA_QK_DIM + EPS) * g for b, s in zip(blks, sums)]
        return [_rope128(y, cos_t, sin_t) for y in normed]

    def put_q(ref, vals):
        if prompt:
            outs = [jnp.transpose(v).astype(BF16) for v in vals]
            for n, o in enumerate(outs):
                ref[n * LANES:(n + 1) * LANES, :] = o
        else:
            for n, v in enumerate(vals):
                ref[:, n * LANES:(n + 1) * LANES] = v.astype(BF16)

    put_q(qd_ref, pair_norm(lane_blocks(_dot(hb, wm_ref[:, _Q0:_K0])), gq_ref[...]))
    kns = pair_norm(lane_blocks(_dot(hb, wm_ref[:, _K0:_V0])), gk_ref[...])
    va = _dot(hb, wm_ref[:, _V0:_CQ0])
    n_tok = x.shape[0]
    for hk, (kn, vh) in enumerate(zip(kns, lane_blocks(va))):
        kd_ref[pl.ds(hk, n_tok, stride=DA_KV_HEADS), :] = kn
        vd_ref[pl.ds(hk, n_tok, stride=DA_KV_HEADS), :] = vh
        if prompt:
            kdb_ref[:, hk * LANES:(hk + 1) * LANES] = kn.astype(BF16)
    if prompt:
        ones_rows = jnp.where(lax.broadcasted_iota(jnp.int32, (SUM_ROWS, x.shape[0]), 0) == 0, 1.0, 0.0).astype(BF16)
        for hk in range(DA_KV_HEADS):
            r0 = hk * (DA_V_DIM + SUM_ROWS)
            vdt_ref[r0:r0 + DA_V_DIM, :] = jnp.transpose(va[:, hk * LANES:(hk + 1) * LANES]).astype(BF16)
            vdt_ref[r0 + DA_V_DIM:r0 + DA_V_DIM + SUM_ROWS, :] = ones_rows

    cq = _dot(hb, wm_ref[:, _CQ0:_CKV0])
    cq = cq * lax.rsqrt(jnp.mean(cq * cq, axis=-1, keepdims=True) + EPS) * gcq_ref[...]
    put_q(qm_ref, head_norm_rope(lane_blocks(_dot(cq.astype(BF16), wuq_ref[...])), gmq_ref[...]))

    ckv = _dot(hb, wm_ref[:, _CKV0:_KR0])
    c = ckv * lax.rsqrt(jnp.mean(ckv * ckv, axis=-1, keepdims=True) + EPS) * gckv_ref[...]
    c_ref[...] = c
    krp = _dot(hb, wm_ref[:, _KR0:_MAIN_COLS])
    kr_ref[...] = krp[:, ROPE_LO:ROPE_LO + MLA_ROPE]

    if prompt:
        cb = c.astype(BF16)
        kn = _dot(cb, wuk_ref[...])
        kms = head_norm_rope([b + krp for b in lane_blocks(kn)], gmk_ref[...])
        for hm, km in enumerate(kms):
            km_ref[:, hm * LANES:(hm + 1) * LANES] = km.astype(BF16)
        vm = _dot(cb, wuv_ref[...])
        for blk in range(MLA_WIDTH // LANES):
            pair = jnp.transpose(vm[:, blk * LANES:(blk + 1) * LANES]).astype(BF16)
            for hh in range(2):
                r0 = (2 * blk + hh) * (MLA_V_DIM + SUM_ROWS)
                vmt_ref[r0:r0 + MLA_V_DIM, :] = pair[hh * MLA_V_DIM:(hh + 1) * MLA_V_DIM, :]
                vmt_ref[r0 + MLA_V_DIM:r0 + MLA_V_DIM + SUM_ROWS, :] = ones_rows


def _project(x2d, w, cos_tab, sin_tab, prompt, batch=None):
    n = x2d.shape[0]
    tm = min(TOKEN_TILE, n)
    assert n % tm == 0 and cos_tab.shape[0] % tm == 0
    n_pos = cos_tab.shape[0] // tm
    row = lambda cols: pl.BlockSpec((tm, cols), lambda i: (i, 0))
    pos = pl.BlockSpec((tm, LANES), lambda i: (i % n_pos, 0))
    consts = [w['norm_g'], w['w_main'], w['gq'], w['gk'], w['gcq'], w['w_uq'], w['gmq'], w['gckv'],
              w['w_uk_pad'], w['w_uv'], w['gmk'], w['halves']]
    sds = jax.ShapeDtypeStruct
    kv_shape = sds((n * DA_KV_HEADS, LANES), F32)
    kv_spec = pl.BlockSpec((tm * DA_KV_HEADS, LANES), lambda i: (i, 0))
    if prompt:
        s = cos_tab.shape[0]
        assert batch * s == n
        tr = lambda rows: pl.BlockSpec((None, rows, tm), lambda i: (i // n_pos, 0, i % n_pos))
        vdt_rows = DA_KV_HEADS * (DA_V_DIM + SUM_ROWS)
        vmt_rows = MLA_HEADS * (MLA_V_DIM + SUM_ROWS)
        out_shape = [sds((batch, DA_WIDTH, s), BF16), kv_shape, sds((n, 512), BF16), kv_shape,
                     sds((batch, vdt_rows, s), BF16), sds((n, MLA_KV_RANK), F32), sds((n, MLA_ROPE), F32),
                     sds((batch, MLA_HEADS * LANES, s), BF16), sds((n, MLA_HEADS * LANES), BF16),
                     sds((batch, vmt_rows, s), BF16)]
        out_specs = [tr(DA_WIDTH), kv_spec, row(512), kv_spec, tr(vdt_rows), row(MLA_KV_RANK), row(MLA_ROPE),
                     tr(MLA_HEADS * LANES), row(MLA_HEADS * LANES), tr(vmt_rows)]
    else:
        out_shape = [sds((n, DA_WIDTH), BF16), kv_shape, kv_shape, sds((n, MLA_KV_RANK), F32),
                     sds((n, MLA_ROPE), F32), sds((n, MLA_HEADS * LANES), BF16)]
        out_specs = [row(DA_WIDTH), kv_spec, kv_spec, row(MLA_KV_RANK), row(MLA_ROPE), row(MLA_HEADS * LANES)]
    return pl.pallas_call(
        functools.partial(_project_body, prompt),
        grid=(n // tm,),
        in_specs=[row(D_MODEL)] + [_const_spec(a.shape) for a in consts] + [pos, pos],
        out_specs=out_specs,
        out_shape=out_shape,
        compiler_params=pltpu.CompilerParams(dimension_semantics=("parallel",), vmem_limit_bytes=VMEM_LIMIT_BYTES),
        name="project_prompt" if prompt else "project_sample",
    )(x2d, *consts, cos_tab, sin_tab)


def _absorb_chunks(chunks, m_ref, acc_ref):
    stats = []
    for s, _, cols in chunks:
        m_old = m_ref[:, cols]
        m_new = jnp.maximum(m_old, jnp.max(s[...], axis=0, keepdims=True))
        m_ref[:, cols] = m_new
        stats.append((m_new, jnp.exp2(m_old - m_new)))
    probs = [jnp.exp2(s[...] - m_new).astype(BF16) for (s, _, _), (m_new, _) in zip(chunks, stats)]
    for (_, vt, cols), (_, alpha), p in zip(chunks, stats, probs):
        acc_ref[:, cols] = alpha * acc_ref[:, cols] + _dot(vt, p)


def _key_tile_pipeline(i, qk, absorb, scores_ready=False, beside_last=None):
    if not scores_ready:
        qk(0, 0)

    def last(r, buf):
        if beside_last is not None:
            beside_last()
        absorb(r, buf)

    @pl.when(i == 0)
    def _():
        last(0, 0)

    @pl.when(i >= 1)
    def _():
        qk(1, 1)
        absorb(0, 0)

        @pl.when(i == 1)
        def _():
            last(1, 1)

        @pl.when(i >= 2)
        def _():
            qk(2, 0)
            absorb(1, 1)
            n_pairs = (i - 2) // 2

            def pair(p, carry):
                r = 2 + 2 * p
                qk(r + 1, 1)
                absorb(r, 0)
                qk(r + 2, 0)
                absorb(r + 1, 1)
                return carry
            lax.fori_loop(0, n_pairs, pair, 0)
            r = 2 + 2 * n_pairs

            @pl.when(r == i)
            def _():
                last(r, 0)

            @pl.when(r != i)
            def _():
                qk(r + 1, 1)
                absorb(r, 0)
                last(r + 1, 1)


def _diff_prompt_body(lam_ref, qt_ref, k_ref, vt_ref, bdiag_ref, bsub_ref, o_ref, *scratch):
    qs_refs, s_refs, m_refs, acc_refs = scratch[0:2], scratch[2:6], scratch[6:8], scratch[8:10]
    i = pl.program_id(2)
    t = ATT_TILE
    ch = Q_CHUNK
    per_g = t // ch
    n_chunks = DA_GROUP * per_g

    row = lax.broadcasted_iota(jnp.int32, (LANES, 1), 0)
    for g in range(DA_GROUP):
        qg = qt_ref[g * LANES:(g + 1) * LANES, :]
        qs_refs[0][:, g * t:(g + 1) * t] = jnp.where(row < DA_HEAD_DIM, qg, jnp.zeros_like(qg))
        qs_refs[1][:, g * t:(g + 1) * t] = jnp.where(row < DA_HEAD_DIM, jnp.zeros_like(qg), qg)
    for mp in range(2):
        m_refs[mp][...] = jnp.full(m_refs[mp].shape, MASKED, F32)
        acc_refs[mp][...] = jnp.zeros(acc_refs[mp].shape, F32)

    def n_keys(r, cc):
        return (cc + 1) * ch if isinstance(r, int) and r == 0 else t

    def stream(mp):
        def qk(r, buf):
            ks = pl.multiple_of((i - r) * t, t)
            for c in range(n_chunks):
                g, cc = divmod(c, per_g)
                nk = n_keys(r, cc)
                cols = slice(c * ch, (c + 1) * ch)
                s = _dot(k_ref[pl.ds(ks, nk), :], qs_refs[mp][:, cols])
                if isinstance(r, int) and r == 0:
                    s = s + bdiag_ref[g, 0:nk, cc * ch:(cc + 1) * ch]
                elif isinstance(r, int) and r == 1 and cc == 0:
                    s = s + bsub_ref[g]
                s_refs[2 * mp + buf][0:nk, cols] = s

        def absorb(r, buf):
            ks = pl.multiple_of((i - r) * t, t)
            chunks = []
            for c in range(n_chunks):
                nk = n_keys(r, c % per_g)
                cols = slice(c * ch, (c + 1) * ch)
                chunks.append((s_refs[2 * mp + buf].at[0:nk, cols], vt_ref[:, pl.ds(ks, nk)], cols))
            _absorb_chunks(chunks, m_refs[mp], acc_refs[mp])
        return qk, absorb

    qk0, absorb0 = stream(0)
    qk1, absorb1 = stream(1)
    _key_tile_pipeline(i, qk0, absorb0, beside_last=lambda: qk1(0, 0))
    _key_tile_pipeline(i, qk1, absorb1, scores_ready=True)

    o0, o1 = [a[0:DA_V_DIM, :] * (1.0 / a[DA_V_DIM:DA_V_DIM + 1, :]) for a in acc_refs]
    comb = o0 - lam_ref[0] * o1
    for g in range(DA_GROUP):
        o_ref[:, g * LANES:(g + 1) * LANES] = jnp.transpose(comb[:, g * t:(g + 1) * t])


def _diff_prompt(lam, qdt, kdb, vdt, bias_diag, bias_sub):
    b, _, s = qdt.shape
    t = ATT_TILE
    assert s % t == 0
    vrows = DA_V_DIM + SUM_ROWS
    return pl.pallas_call(
        _diff_prompt_body,
        grid=(b, DA_KV_HEADS, s // t),
        in_specs=[pl.BlockSpec(memory_space=pltpu.SMEM),
                  pl.BlockSpec((None, DA_GROUP * LANES, t), lambda bi, h, i: (bi, h, i)),
                  pl.BlockSpec((None, s, LANES), lambda bi, h, i: (bi, 0, h)),
                  pl.BlockSpec((None, vrows, s), lambda bi, h, i: (bi, h, 0)),
                  pl.BlockSpec((DA_GROUP, t, t), lambda bi, h, i: (h, 0, 0)),
                  pl.BlockSpec((DA_GROUP, t, Q_CHUNK), lambda bi, h, i: (h, 0, 0))],
        out_specs=pl.BlockSpec((None, t, DA_GROUP * LANES), lambda bi, h, i: (bi, i, h)),
        out_shape=jax.ShapeDtypeStruct((b, s, DA_WIDTH), F32),
        scratch_shapes=([pltpu.VMEM((LANES, DA_GROUP * t), BF16)] * 2
                        + [pltpu.VMEM((t, DA_GROUP * t), F32)] * 4
                        + [pltpu.VMEM((1, DA_GROUP * t), F32)] * 2
                        + [pltpu.VMEM((vrows, DA_GROUP * t), F32)] * 2),
        compiler_params=pltpu.CompilerParams(
            dimension_semantics=("parallel", "parallel", "arbitrary"), vmem_limit_bytes=VMEM_LIMIT_BYTES),
        name="diff_prompt",
    )(lam, qdt, kdb, vdt, bias_diag, bias_sub)


def _mla_prompt_body(qt_ref, k_ref, vt_ref, o_ref, *scratch):
    s_refs, m_refs, acc_refs = scratch[0:4], scratch[4:6], scratch[6:8]
    i = pl.program_id(2)
    t = ATT_TILE
    ch = Q_CHUNK
    n_chunks = t // ch
    vrows = MLA_V_DIM + SUM_ROWS
    scale2 = (MLA_QK_DIM ** -0.5) * LOG2E
    for hh in range(2):
        m_refs[hh][...] = jnp.full(m_refs[hh].shape, MASKED, F32)
        acc_refs[hh][...] = jnp.zeros(acc_refs[hh].shape, F32)

    def n_keys(r, c):
        return (c + 1) * ch if isinstance(r, int) and r == 0 else t

    def stream(hh):
        hs = slice(hh * LANES, (hh + 1) * LANES)
        vs = slice(hh * vrows, (hh + 1) * vrows)

        def qk(r, buf):
            ks = pl.multiple_of((i - r) * t, t)
            for c in range(n_chunks):
                nk = n_keys(r, c)
                cols = slice(c * ch, (c + 1) * ch)
                s = _dot(k_ref[pl.ds(ks, nk), hs], qt_ref[hs, cols]) * scale2
                if isinstance(r, int) and r == 0:
                    kidx = lax.broadcasted_iota(jnp.int32, (nk, ch), 0)
                    qidx = lax.broadcasted_iota(jnp.int32, (nk, ch), 1) + c * ch
                    s = jnp.where(kidx <= qidx, s, MASKED)
                s_refs[2 * hh + buf][0:nk, cols] = s

        def absorb(r, buf):
            ks = pl.multiple_of((i - r) * t, t)
            chunks = []
            for c in range(n_chunks):
                nk = n_keys(r, c)
                cols = slice(c * ch, (c + 1) * ch)
                chunks.append((s_refs[2 * hh + buf].at[0:nk, cols], vt_ref[vs, pl.ds(ks, nk)], cols))
            _absorb_chunks(chunks, m_refs[hh], acc_refs[hh])
        return qk, absorb

    qk0, absorb0 = stream(0)
    qk1, absorb1 = stream(1)
    _key_tile_pipeline(i, qk0, absorb0, beside_last=lambda: qk1(0, 0))
    _key_tile_pipeline(i, qk1, absorb1, scores_ready=True)

    outs = [a[0:MLA_V_DIM, :] * (1.0 / a[MLA_V_DIM:MLA_V_DIM + 1, :]) for a in acc_refs]
    o_ref[...] = jnp.transpose(jnp.concatenate(outs, axis=0))


def _mla_prompt(qmt, km, vmt):
    b, _, s = qmt.shape
    t = ATT_TILE
    vrows = MLA_V_DIM + SUM_ROWS
    return pl.pallas_call(
        _mla_prompt_body,
        grid=(b, MLA_HEADS // 2, s // t),
        in_specs=[pl.BlockSpec((None, 2 * LANES, t), lambda bi, hp, i: (bi, hp, i)),
                  pl.BlockSpec((None, s, 2 * LANES), lambda bi, hp, i: (bi, 0, hp)),
                  pl.BlockSpec((None, 2 * vrows, s), lambda bi, hp, i: (bi, hp, 0))],
        out_specs=pl.BlockSpec((None, t, 2 * MLA_V_DIM), lambda bi, hp, i: (bi, i, hp)),
        out_shape=jax.ShapeDtypeStruct((b, s, MLA_WIDTH), F32),
        scratch_shapes=([pltpu.VMEM((t, t), F32)] * 4
                        + [pltpu.VMEM((1, t), F32)] * 2
                        + [pltpu.VMEM((vrows, t), F32)] * 2),
        compiler_params=pltpu.CompilerParams(
            dimension_semantics=("parallel", "parallel", "arbitrary"), vmem_limit_bytes=VMEM_LIMIT_BYTES),
        name="mla_prompt",
    )(qmt, km, vmt)


def _sample_body(n_seq, n_steps, pt_ref, lam_ref, ck_hbm, cv_hbm, cc_hbm, cr_hbm,
                 kn_ref, vn_ref, cn_ref, rn_ref, qbd_ref, qmn_ref, qrr_ref, gkn_ref, wuk_ref, wuv_ref, eseg_ref,
                 bias_ref, mmask_ref, ctab_ref, stab_ref,
                 oa_ref, ob_ref,
                 qabs_ref, md_ref, ld_ref, accd_ref, mm_ref, lm_ref, accm_ref, kbuf, vbuf, cbuf, rbuf, sems):
    pg = PAGES_PER_STEP
    b = pl.program_id(0)
    j = pl.program_id(1)
    scale_m = (MLA_QK_DIM ** -0.5) * LOG2E
    page = cc_hbm.shape[1]
    krows = page * DA_KV_HEADS
    t_new = cn_ref.shape[0]
    n_mla = MLA_HEADS * t_new

    def page_copies(pid, p, slot):
        return (pltpu.make_async_copy(ck_hbm.at[pid], kbuf.at[slot, pl.ds(p * krows, krows), :], sems.at[slot, 0]),
                pltpu.make_async_copy(cv_hbm.at[pid], vbuf.at[slot, pl.ds(p * krows, krows), :], sems.at[slot, 1]),
                pltpu.make_async_copy(cc_hbm.at[pid], cbuf.at[slot, pl.ds(p * page, page), :], sems.at[slot, 2]),
                pltpu.make_async_copy(cr_hbm.at[pid], rbuf.at[slot, :, pl.ds(p * page, page)], sems.at[slot, 3]))

    def start_step(bb, jj, slot):
        for p in range(pg):
            for cp in page_copies(pt_ref[bb, jj * pg + p], p, slot):
                cp.start()

    def wait_step(slot):
        for p in range(pg):
            for cp in page_copies(0, p, slot):
                cp.wait()

    g = b * n_steps + j
    n_global = n_seq * n_steps
    slot = lax.rem(g, PAGE_SLOTS)
    ahead = PAGE_SLOTS - 1

    @pl.when(g == 0)
    def _():
        for s0 in range(min(ahead, n_global)):
            start_step(s0 // n_steps, s0 % n_steps, s0 % PAGE_SLOTS)

    @pl.when(g + ahead < n_global)
    def _():
        nxt = g + ahead
        start_step(nxt // n_steps, lax.rem(nxt, n_steps), lax.rem(nxt, PAGE_SLOTS))

    wait_step(slot)

    def heads_to_lanes(ref2, start=0, t=None):
        t = ref2.shape[0] // DA_KV_HEADS if t is None else t
        return jnp.concatenate([ref2[pl.ds(start + h, t, stride=DA_KV_HEADS), :].astype(BF16)
                                for h in range(DA_KV_HEADS)], axis=1)

    def rows_from_lanes(alpha, n_rows):
        full = jnp.broadcast_to(alpha, (LANES, alpha.shape[1]))
        if alpha.shape[1] < LANES:
            full = jnp.concatenate([full, jnp.zeros((LANES, LANES - alpha.shape[1]), F32)], axis=1)
        return jnp.transpose(full)[:n_rows, :]

    def score_matmuls(kb, c, krt, ctab, stab):
        cb = c.astype(BF16)
        s = _dot(kb, qbd_ref[...])
        kn = _dot(cb, wuk_ref[...])
        raw = _dot(cb, qabs_ref[...])
        ext = _dot_tn(jnp.concatenate([krt * ctab, krt * stab, krt * krt], axis=0).astype(BF16), qrr_ref[...])
        return cb, s, kn, raw, ext

    def local_softmaxes(blocks):
        half = MLA_HEADS * NOPE_HALF
        ss = []
        for (cb, s, kn, raw, ext), _, _, _ in blocks:
            kn2 = kn * kn
            ss.append(_dot((kn2[:, :half] + kn2[:, half:]).astype(BF16), eseg_ref[...]))
        sa = [s + bias for (_, s, _, _, _), _, bias, _ in blocks]
        ma = [jnp.max(s, axis=0, keepdims=True) for s in sa]
        pa = [jnp.exp2(s - m) for s, m in zip(sa, ma)]
        sb = []
        for ((_, _, _, raw, ext), _, _, mmask), q in zip(blocks, ss):
            r = lax.rsqrt((q + pltpu.roll(ext, LANES - n_mla, 1)[:, :n_mla]) / MLA_QK_DIM + EPS)
            sm = (raw + ext[:, :n_mla]) * r * scale_m
            sb.append(sm if mmask is None else sm + mmask)
        mb = [jnp.max(s, axis=0, keepdims=True) for s in sb]
        pb_ = [jnp.exp2(s - m) for s, m in zip(sb, mb)]
        parts_a = [(m, jnp.sum(p, axis=0, keepdims=True), _dot_tn(p.astype(BF16), vb))
                   for m, p, (_, vb, _, _) in zip(ma, pa, blocks)]
        parts_b = [(m, jnp.sum(p, axis=0, keepdims=True), _dot_tn(p.astype(BF16), pre[0]))
                   for m, p, (pre, _, _, _) in zip(mb, pb_, blocks)]
        return parts_a, parts_b

    def fold(m_ref, l_ref, acc_ref, parts):
        n_rows, width = acc_ref.shape
        m_old = m_ref[...]
        m_new = m_old
        for m, _, _ in parts:
            m_new = jnp.maximum(m_new, m)
        wide = lambda f: jnp.concatenate([rows_from_lanes(f, n_rows)] * (width // LANES), axis=1)
        alpha = jnp.exp2(m_old - m_new)
        l_new = alpha * l_ref[...]
        acc = wide(alpha) * acc_ref[...]
        for m, l, a in parts:
            f = jnp.exp2(m - m_new)
            l_new = l_new + f * l
            acc = acc + wide(f) * a
        m_ref[...] = m_new
        l_ref[...] = l_new
        acc_ref[...] = acc

    @pl.when(j == 0)
    def _():
        qg = (qmn_ref[...].astype(F32) * gkn_ref[...]).astype(BF16)
        qabs_ref[...] = _dot(wuk_ref[...], qg).astype(BF16)
        past = n_steps * pg * page
        rows_n = pl.ds(past, t_new)
        pre = score_matmuls(heads_to_lanes(kn_ref), cn_ref[...], rn_ref[...], ctab_ref[:, rows_n], stab_ref[:, rows_n])
        ((m_d, l_d, pv),), ((m_m, l_m, pc),) = local_softmaxes(
            [(pre, heads_to_lanes(vn_ref), bias_ref[rows_n, :], mmask_ref[...])])
        md_ref[...] = m_d
        ld_ref[...] = l_d
        accd_ref[...] = pv
        mm_ref[...] = m_m
        lm_ref[...] = l_m
        accm_ref[...] = pc

    base = j * (pg * page)
    pb = PAGES_PER_BLOCK
    blocks = []
    tb = pb * page
    for blk in range(pg // pb):
        rows = pl.ds(pl.multiple_of(base + blk * tb, tb), tb)
        keys = slice(blk * tb, (blk + 1) * tb)
        pre = score_matmuls(heads_to_lanes(kbuf.at[slot], blk * tb * DA_KV_HEADS, tb), cbuf[slot, keys, :],
                            rbuf[slot, :, keys], ctab_ref[:, rows], stab_ref[:, rows])
        blocks.append((pre, heads_to_lanes(vbuf.at[slot], blk * tb * DA_KV_HEADS, tb), bias_ref[rows, :], None))
    parts_d, parts_m = local_softmaxes(blocks)
    fold(md_ref, ld_ref, accd_ref, parts_d)
    fold(mm_ref, lm_ref, accm_ref, parts_m)

    @pl.when(j == n_steps - 1)
    def _():
        lam = lam_ref[0]
        inv_d = rows_from_lanes(1.0 / ld_ref[...], LANES)
        accd = accd_ref[...]
        lanes_per_kv = 2 * DA_GROUP * t_new
        for h in range(DA_KV_HEADS):
            cols = slice(h * LANES, (h + 1) * LANES)
            for g in range(DA_GROUP):
                r1 = h * lanes_per_kv + g * t_new
                r2 = r1 + DA_GROUP * t_new
                o1 = accd[r1:r1 + t_new, cols] * inv_d[r1:r1 + t_new, :]
                o2 = accd[r2:r2 + t_new, cols] * inv_d[r2:r2 + t_new, :]
                hq = h * DA_GROUP + g
                oa_ref[:, hq * LANES:(hq + 1) * LANES] = o1 - lam * o2
        inv_m = rows_from_lanes(1.0 / lm_ref[...], n_mla)
        cbar = (accm_ref[...] * jnp.concatenate([inv_m] * (MLA_KV_RANK // LANES), axis=1)).astype(BF16)
        full = _dot(cbar, wuv_ref[...])
        lane5 = lax.broadcasted_iota(jnp.int32, (1, MLA_WIDTH), 1)
        ob = jnp.zeros((t_new, MLA_WIDTH), F32)
        for h in range(MLA_HEADS):
            sel = (lane5 >= h * MLA_V_DIM) & (lane5 < (h + 1) * MLA_V_DIM)
            ob = ob + jnp.where(sel, full[h * t_new:(h + 1) * t_new, :], 0.0)
        ob_ref[...] = ob


def _sample_attention(lam, page_table, ck, cv, cc, crt, new, qbd, qmn, qrr, w, bias_tab, mla_mask, ctab, stab):
    n_seq, n_pages = page_table.shape
    pool, page = cc.shape[0], cc.shape[1]
    t_new = new['c'].shape[1]
    pg = PAGES_PER_STEP
    assert n_pages % pg == 0 and pg % PAGES_PER_BLOCK == 0 and MLA_HEADS * t_new <= LANES // 2
    n_steps = n_pages // pg

    def seq_spec(*tail):
        zeros = (0,) * len(tail)
        return pl.BlockSpec((None,) + tail, lambda b, j, pt: (b,) + zeros)

    def const(shape):
        nd = len(shape)
        return pl.BlockSpec(shape, lambda b, j, pt: (0,) * nd, pipeline_mode=pl.Buffered(1))

    in_specs = [pl.BlockSpec(memory_space=pltpu.SMEM)] + [pl.BlockSpec(memory_space=pl.ANY)] * 4
    per_seq = [new['kd'], new['vd'], new['c'], new['krt'], qbd, qmn, qrr]
    in_specs += [seq_spec(*a.shape[1:]) for a in per_seq]
    consts = [w['gk_nope_col'], w['w_uk'], w['w_uv'], w['eseg'], bias_tab, mla_mask, ctab, stab]
    in_specs += [const(a.shape) for a in consts]
    operands = [ck, cv, cc, crt] + per_seq + consts
    krows = page * DA_KV_HEADS
    grid_spec = pltpu.PrefetchScalarGridSpec(
        num_scalar_prefetch=1,
        grid=(n_seq, n_steps),
        in_specs=in_specs,
        out_specs=[pl.BlockSpec((None, t_new, DA_WIDTH), lambda b, j, pt: (b, 0, 0)),
                   pl.BlockSpec((None, t_new, MLA_WIDTH), lambda b, j, pt: (b, 0, 0))],
        scratch_shapes=[pltpu.VMEM((MLA_KV_RANK, MLA_HEADS * t_new), BF16),
                        pltpu.VMEM((1, LANES), F32), pltpu.VMEM((1, LANES), F32),
                        pltpu.VMEM((LANES, DA_KV_HEADS * DA_V_DIM), F32),
                        pltpu.VMEM((1, MLA_HEADS * t_new), F32), pltpu.VMEM((1, MLA_HEADS * t_new), F32),
                        pltpu.VMEM((MLA_HEADS * t_new, MLA_KV_RANK), F32),
                        pltpu.VMEM((PAGE_SLOTS, pg * krows, LANES), F32),
                        pltpu.VMEM((PAGE_SLOTS, pg * krows, DA_V_DIM), F32),
                        pltpu.VMEM((PAGE_SLOTS, pg * page, MLA_KV_RANK), F32),
                        pltpu.VMEM((PAGE_SLOTS, MLA_ROPE, pg * page), F32),
                        pltpu.SemaphoreType.DMA((PAGE_SLOTS, 4))])
    return pl.pallas_call(
        functools.partial(_sample_body, n_seq, n_steps),
        grid_spec=grid_spec,
        out_shape=[jax.ShapeDtypeStruct((n_seq, t_new, DA_WIDTH), F32),
                   jax.ShapeDtypeStruct((n_seq, t_new, MLA_WIDTH), F32)],
        compiler_params=pltpu.CompilerParams(dimension_semantics=("arbitrary", "arbitrary"),
                                             vmem_limit_bytes=VMEM_LIMIT_BYTES),
        name="sample_attention",
    )(page_table, lam, *operands)


def _merge_body(one_minus_lam_init, x_ref, oa_ref, ob_ref, ng_ref, wzg_ref, gout_ref, wpa_ref, wpb_ref, wo_ref, y_ref):
    x = x_ref[...]
    h = x * lax.rsqrt(jnp.mean(x * x, axis=-1, keepdims=True) + EPS) * ng_ref[...]
    hb = h.astype(BF16)
    za = _dot(hb, wzg_ref[:, 0:DA_WIDTH])
    gout = gout_ref[...]
    parts = []
    for hq in range(DA_HEADS):
        sl = slice(hq * LANES, (hq + 1) * LANES)
        o = oa_ref[:, sl]
        o = o * lax.rsqrt(jnp.mean(o * o, axis=-1, keepdims=True) + EPS) * gout * one_minus_lam_init
        z = za[:, sl]
        parts.append((o * (z * jax.nn.sigmoid(z))).astype(BF16))
    ya = _dot(jnp.concatenate(parts, axis=1), wpa_ref[...])
    zb = _dot(hb, wzg_ref[:, DA_WIDTH:DA_WIDTH + MLA_WIDTH])
    yb = _dot((ob_ref[...] * (zb * jax.nn.sigmoid(zb))).astype(BF16), wpb_ref[...])
    g0 = DA_WIDTH + MLA_WIDTH
    ga = _dot(hb, wzg_ref[:, g0:g0 + D_MODEL])
    gb = _dot(hb, wzg_ref[:, g0 + D_MODEL:g0 + 2 * D_MODEL])
    mix = jax.nn.sigmoid(ga) * ya + jax.nn.sigmoid(gb) * yb
    y_ref[...] = x + _dot(mix.astype(BF16), wo_ref[...])


def _merge(x2d, oa, ob, w, lam_init):
    n = x2d.shape[0]
    tm = min(TOKEN_TILE, n)
    assert n % tm == 0
    row = lambda cols: pl.BlockSpec((tm, cols), lambda i: (i, 0))
    consts = [w['norm_g'], w['w_zg'], w['gout'], w['w_pa'], w['w_pb'], w['w_o']]
    return pl.pallas_call(
        functools.partial(_merge_body, 1.0 - lam_init),
        grid=(n // tm,),
        in_specs=[row(D_MODEL), row(DA_WIDTH), row(MLA_WIDTH)] + [_const_spec(a.shape) for a in consts],
        out_specs=row(D_MODEL),
        out_shape=jax.ShapeDtypeStruct((n, D_MODEL), F32),
        compiler_params=pltpu.CompilerParams(dimension_semantics=("parallel",), vmem_limit_bytes=VMEM_LIMIT_BYTES),
        name="merge",
    )(x2d, oa, ob, *consts)


def _rel_bucket(dist):
    n = jnp.maximum(dist, 0)
    max_exact = REL_BUCKETS // 2
    nf = jnp.maximum(n, 1).astype(F32)
    large = max_exact + (jnp.log(nf / max_exact) / math.log(REL_MAX_DIST / max_exact)
                         * (REL_BUCKETS - max_exact)).astype(jnp.int32)
    large = jnp.minimum(large, REL_BUCKETS - 1)
    return jnp.where(n < max_exact, n, large)


def _rope_angles(pos):
    inv = jnp.power(ROPE_THETA, -jnp.arange(HALF, dtype=F32) / HALF)
    ang = pos.astype(F32)[:, None] * inv[None, :]
    return jnp.cos(ang), jnp.sin(ang)


def _rope_tables128(pos):
    cos, sin = _rope_angles(pos)
    n = pos.shape[0]
    cos_t = jnp.concatenate([jnp.ones((n, MLA_NOPE), F32), cos, cos, jnp.zeros((n, LANES - MLA_QK_DIM), F32)], axis=1)
    sin_t = jnp.concatenate([jnp.zeros((n, MLA_NOPE), F32), -sin, sin, jnp.zeros((n, LANES - MLA_QK_DIM), F32)], axis=1)
    return cos_t, sin_t


def _pad_heads(a, live):
    pad = [(0, 0)] * (a.ndim - 1) + [(0, LANES - live)]
    a = jnp.pad(a, pad)
    return a.reshape(a.shape[:-2] + (a.shape[-2] * LANES,))


def _prep_layer(p, t_new):
    offs = [int(v) for v in np.cumsum((0,) + IN_SPLITS)]
    wq, wk, wv, wza, wcq, wckv, wkr, wzb, wga, wgb = [p['w_in'][:, offs[n]:offs[n + 1]] for n in range(10)]
    with_x1_copy = lambda a: jnp.concatenate([a, a[..., MLA_NOPE:MLA_NOPE + HALF]], axis=-1)
    tail = LANES - MLA_QK_DIM - HALF
    wkr_placed = jnp.pad(jnp.concatenate([wkr, wkr[:, :HALF]], axis=1), ((0, 0), (ROPE_LO, tail)))
    halves = np.arange(LANES) // DA_HEAD_DIM
    w_ukv = p['w_ukv'].reshape(MLA_KV_RANK, MLA_HEADS, MLA_NOPE + MLA_V_DIM)
    w_uk = w_ukv[:, :, :MLA_NOPE]
    w_uk_halves = jnp.transpose(w_uk.reshape(MLA_KV_RANK, MLA_HEADS, 2, NOPE_HALF), (0, 2, 1, 3))
    gmk = p['mla_k_norm']
    gk_nope = jnp.broadcast_to(gmk[:MLA_NOPE].reshape(2, 1, NOPE_HALF), (2, MLA_HEADS, NOPE_HALF))
    head_of_row = np.arange(MLA_HEADS * NOPE_HALF) // NOPE_HALF
    head_of_lane = np.arange(MLA_HEADS * t_new) // t_new
    return dict(
        norm_g=p['norm_g'].reshape(1, D_MODEL),
        w_main=jnp.concatenate([wq, wk, wv, wcq, wckv, wkr_placed], axis=1).astype(BF16),
        w_zg=jnp.concatenate([wza, wzb, wga, wgb], axis=1).astype(BF16),
        gq=p['da_q_norm'].reshape(1, LANES) * ((DA_HEAD_DIM ** -0.5) * LOG2E),
        gk=p['da_k_norm'].reshape(1, LANES),
        gcq=p['mla_cq_norm'].reshape(1, MLA_Q_RANK),
        w_uq=_pad_heads(with_x1_copy(p['w_uq'].reshape(MLA_Q_RANK, MLA_HEADS, MLA_QK_DIM)),
                        MLA_QK_DIM + HALF).astype(BF16),
        gmq=jnp.pad(with_x1_copy(p['mla_q_norm']), (0, tail)).reshape(1, LANES),
        halves=jnp.asarray(halves[:, None] == halves[None, :], BF16),
        gckv=p['mla_ckv_norm'].reshape(1, MLA_KV_RANK),
        w_uk_pad=_pad_heads(w_uk, MLA_NOPE).astype(BF16),
        w_uk=w_uk_halves.reshape(MLA_KV_RANK, MLA_HEADS * MLA_NOPE).astype(BF16),
        w_uv=w_ukv[:, :, MLA_NOPE:].reshape(MLA_KV_RANK, MLA_WIDTH).astype(BF16),
        gmk=jnp.pad(with_x1_copy(gmk), (0, tail)).reshape(1, LANES),
        gk_nope_col=gk_nope.reshape(MLA_HEADS * MLA_NOPE, 1),
        gk_rope=gmk[MLA_NOPE:],
        eseg=jnp.asarray(head_of_row[:, None] == head_of_lane[None, :], BF16),
        gout=p['da_out_norm'].reshape(1, DA_V_DIM),
        w_pa=p['w_pa'].astype(BF16), w_pb=p['w_pb'].astype(BF16), w_o=p['w_o'].astype(BF16),
    )


def _toeplitz(v, n):
    lead = v.shape[:-1]
    u = jnp.flip(v, axis=-1)
    w = jnp.concatenate([u, jnp.zeros(lead + (1,), v.dtype)], axis=-1)
    flat = jnp.tile(w, (1,) * len(lead) + (n,))[..., :n * (2 * n - 1)]
    return flat.reshape(lead + (n, 2 * n - 1))[..., n - 1:]


def _prompt_bias_tiles(rel_bias):
    t = ATT_TILE
    far = rel_bias[REL_BUCKETS - 1]
    d = jnp.arange(2 * t - 1, dtype=jnp.int32) - (t - 1)
    f = jnp.where((d >= 0)[:, None], rel_bias[_rel_bucket(d)] - far[None, :], MASKED)
    diag = _toeplitz(f.T, t)
    dc = LANES + jnp.arange(2 * LANES - 1, dtype=jnp.int32) - (LANES - 1)
    corner = _toeplitz((rel_bias[_rel_bucket(dc)] - far[None, :]).T, LANES)
    sub = jnp.pad(jnp.swapaxes(corner, 1, 2), ((0, 0), (t - LANES, 0), (0, Q_CHUNK - LANES)))
    return jnp.swapaxes(diag, 1, 2) * LOG2E, sub * LOG2E


def _sample_tables(rel_bias, gk_rope, past, t_new):
    n_keys = past + t_new
    near = REL_MAX_DIST + t_new
    k_pos = jnp.arange(n_keys - near, n_keys, dtype=jnp.int32)
    q_pos = past + jnp.arange(t_new, dtype=jnp.int32)
    b = rel_bias[_rel_bucket(q_pos[None, :] - k_pos[:, None])]
    b = b - rel_bias[REL_BUCKETS - 1][None, None, :]
    b = jnp.where((q_pos[None, :] >= k_pos[:, None])[:, :, None], b, MASKED)
    b = jnp.transpose(b.reshape(near, t_new, DA_KV_HEADS, DA_GROUP), (0, 2, 3, 1))
    b = jnp.broadcast_to(b[:, :, None], (near, DA_KV_HEADS, 2, DA_GROUP, t_new)).reshape(near, 2 * DA_HEADS * t_new)
    b = jnp.pad(b.astype(F32) * LOG2E, ((n_keys - near, 0), (0, 0)))
    newk = jnp.arange(t_new)
    mm = jnp.where(newk[None, :] >= newk[:, None], 0.0, MASKED).astype(F32)
    mm = jnp.tile(mm, (1, MLA_HEADS))
    cos, sin = _rope_angles(jnp.arange(n_keys, dtype=jnp.int32))
    ctab = jnp.concatenate([cos, cos], axis=1) * gk_rope[None, :]
    stab = jnp.concatenate([sin, sin], axis=1) * gk_rope[None, :]
    return b, mm, ctab.T, stab.T


def _sample_query_layouts(qd, qm, n_seq, t_new):
    q = qd.reshape(n_seq, t_new, DA_KV_HEADS, DA_GROUP, 2, DA_HEAD_DIM)
    q = jnp.transpose(q, (0, 2, 4, 5, 3, 1)).reshape(n_seq, DA_KV_HEADS * 2, DA_HEAD_DIM, DA_GROUP * t_new)
    eye = jnp.eye(DA_KV_HEADS * 2, dtype=qd.dtype)
    qbd = q[:, :, :, None, :] * eye[None, :, None, :, None]
    qbd = qbd.reshape(n_seq, DA_KV_HEADS * 2 * DA_HEAD_DIM, DA_KV_HEADS * 2 * DA_GROUP * t_new)
    qh = qm.reshape(n_seq, t_new, MLA_HEADS, LANES)
    qn = jnp.transpose(qh[..., :MLA_NOPE].reshape(n_seq, t_new, MLA_HEADS, 2, NOPE_HALF), (0, 3, 2, 4, 1))
    eye_h = jnp.eye(MLA_HEADS, dtype=qm.dtype)
    qmn = (qn[:, :, :, :, None, :] * eye_h[None, None, :, None, :, None]).reshape(
        n_seq, MLA_HEADS * MLA_NOPE, MLA_HEADS * t_new)
    qr = jnp.transpose(qh[..., ROPE_LO:ROPE_LO + MLA_ROPE], (0, 3, 2, 1)).reshape(n_seq, MLA_ROPE, MLA_HEADS * t_new)
    qrs = jnp.concatenate([qr[:, HALF:], -qr[:, :HALF]], axis=1)
    n_mla = MLA_HEADS * t_new
    top = jnp.pad(jnp.concatenate([qr, qrs], axis=1), ((0, 0), (0, 0), (0, LANES - n_mla)))
    bottom = jnp.pad(jnp.ones((n_seq, MLA_ROPE, n_mla), qm.dtype), ((0, 0), (0, 0), (LANES - n_mla, 0)))
    return qbd, qmn, jnp.concatenate([top, bottom], axis=1)


def kernel(x_prompt, x_sample, cache_diff_k, cache_diff_v, cache_mla_latent, cache_mla_krope, page_table, rel_bias,
           norm_g, w_in, da_q_norm, da_k_norm, lam_q1, lam_k1, lam_q2, lam_k2, da_out_norm, w_pa, mla_cq_norm, w_uq,
           mla_ckv_norm, w_ukv, mla_q_norm, mla_k_norm, w_pb, w_o):
    b, s, _ = x_prompt.shape
    n_seq, t_new, _ = x_sample.shape
    depth, pool, page = cache_diff_k.shape[:3]
    past = page_table.shape[1] * page
    page_table = page_table.astype(jnp.int32)

    cos_p, sin_p = _rope_tables128(jnp.arange(s, dtype=jnp.int32))
    rows_s = min(TOKEN_TILE, n_seq * t_new)
    cos_s, sin_s = _rope_tables128(past + (jnp.arange(rows_s, dtype=jnp.int32) % t_new))
    bias_diag, bias_sub = _prompt_bias_tiles(rel_bias)

    xp = x_prompt.reshape(b * s, D_MODEL)
    xs = x_sample.reshape(n_seq * t_new, D_MODEL)
    news = [[] for _ in range(8)]
    for l in range(depth):
        p = dict(norm_g=norm_g[l], w_in=w_in[l], da_q_norm=da_q_norm[l], da_k_norm=da_k_norm[l],
                 da_out_norm=da_out_norm[l], w_pa=w_pa[l], mla_cq_norm=mla_cq_norm[l], w_uq=w_uq[l],
                 mla_ckv_norm=mla_ckv_norm[l], w_ukv=w_ukv[l], mla_q_norm=mla_q_norm[l],
                 mla_k_norm=mla_k_norm[l], w_pb=w_pb[l], w_o=w_o[l])
        w = _prep_layer(p, t_new)
        lam_init = 0.8 - 0.6 * math.exp(-0.3 * l)
        lam = (jnp.exp(jnp.sum(lam_q1[l] * lam_k1[l]).astype(F32))
               - jnp.exp(jnp.sum(lam_q2[l] * lam_k2[l]).astype(F32)) + lam_init).reshape(1)

        qdt, kd, kdb, vd, vdt, c, kr, qmt, km, vmt = _project(xp, w, cos_p, sin_p, True, batch=b)
        oa_p = _diff_prompt(lam, qdt, kdb.reshape(b, s, 512), vdt, bias_diag, bias_sub)
        ob_p = _mla_prompt(qmt, km.reshape(b, s, MLA_HEADS * LANES), vmt)

        qd_s, kd_s, vd_s, c_s, kr_s, qm_s = _project(xs, w, cos_s, sin_s, False)
        qbd, qmn, qrr = _sample_query_layouts(qd_s, qm_s, n_seq, t_new)
        bias_tab, mla_mask, ctab, stab = _sample_tables(rel_bias, w['gk_rope'], past, t_new)
        new = dict(kd=kd_s.reshape(n_seq, t_new * DA_KV_HEADS, LANES), vd=vd_s.reshape(n_seq, t_new * DA_KV_HEADS, DA_V_DIM),
                   c=c_s.reshape(n_seq, t_new, MLA_KV_RANK),
                   krt=jnp.swapaxes(kr_s.reshape(n_seq, t_new, MLA_ROPE), 1, 2))
        oa_s, ob_s = _sample_attention(
            lam, page_table, cache_diff_k[l].reshape(pool, page * DA_KV_HEADS, LANES),
            cache_diff_v[l].reshape(pool, page * DA_KV_HEADS, DA_V_DIM), cache_mla_latent[l],
            jnp.swapaxes(cache_mla_krope[l], 1, 2), new, qbd, qmn, qrr, w, bias_tab, mla_mask, ctab, stab)

        xp = _merge(xp, oa_p.reshape(b * s, DA_WIDTH), ob_p.reshape(b * s, MLA_WIDTH), w, lam_init)
        xs = _merge(xs, oa_s.reshape(n_seq * t_new, DA_WIDTH), ob_s.reshape(n_seq * t_new, MLA_WIDTH), w, lam_init)

        for lst, a in zip(news, (kd.reshape(b, s, DA_KV_HEADS, LANES), vd.reshape(b, s, DA_KV_HEADS, DA_V_DIM),
                                 c.reshape(b, s, MLA_KV_RANK), kr.reshape(b, s, MLA_ROPE),
                                 kd_s.reshape(n_seq, t_new, DA_KV_HEADS, LANES),
                                 vd_s.reshape(n_seq, t_new, DA_KV_HEADS, DA_V_DIM),
                                 c_s.reshape(n_seq, t_new, MLA_KV_RANK), kr_s.reshape(n_seq, t_new, MLA_ROPE))):
            lst.append(a)
    return (xp.reshape(b, s, D_MODEL), xs.reshape(n_seq, t_new, D_MODEL)) + tuple(jnp.stack(a) for a in news)
```

```python
import functools
import math

import numpy as np
import jax
import jax.numpy as jnp
from jax import lax
from jax.experimental import pallas as pl
from jax.experimental.pallas import tpu as pltpu

F32 = jnp.float32
BF16 = jnp.bfloat16

D_MODEL = 1024
DA_HEADS = 8
DA_KV_HEADS = 4
DA_GROUP = DA_HEADS // DA_KV_HEADS
DA_HEAD_DIM = 64
DA_V_DIM = 2 * DA_HEAD_DIM
DA_WIDTH = DA_HEADS * DA_V_DIM
MLA_HEADS = 8
MLA_Q_RANK = 384
MLA_KV_RANK = 256
MLA_NOPE = 64
MLA_ROPE = 32
MLA_QK_DIM = MLA_NOPE + MLA_ROPE
MLA_V_DIM = 64
MLA_WIDTH = MLA_HEADS * MLA_V_DIM
ROPE_THETA = 10000.0
REL_BUCKETS = 32
REL_MAX_DIST = 128
EPS = 1e-6
IN_SPLITS = (DA_HEADS * 2 * DA_HEAD_DIM, DA_KV_HEADS * 2 * DA_HEAD_DIM, DA_KV_HEADS * DA_V_DIM, DA_WIDTH,
             MLA_Q_RANK, MLA_KV_RANK, MLA_ROPE, MLA_WIDTH, D_MODEL, D_MODEL)

LANES = 128
VMEM_LIMIT_BYTES = 56 * 1024 * 1024

MASKED = -1e30
HALF = MLA_ROPE // 2
ROPE_LO = MLA_NOPE
NOPE_HALF = MLA_NOPE // 2
SUM_ROWS = 16
LOG2E = math.log2(math.e)

_Q0, _K0, _V0, _CQ0, _CKV0, _KR0, _MAIN_COLS = 0, 1024, 1536, 2048, 2432, 2688, 2816

TOKEN_TILE = 512
ATT_TILE = 512
Q_CHUNK = 256
PAGES_PER_STEP = 8
PAGES_PER_BLOCK = 2
PAGE_SLOTS = 3


def _const_spec(shape):
    nd = len(shape)
    return pl.BlockSpec(shape, lambda *_: (0,) * nd, pipeline_mode=pl.Buffered(1))


def _dot(a, b):
    return jnp.dot(a, b, preferred_element_type=F32)


def _dot_tn(a, b):
    return lax.dot_general(a, b, (((0,), (0,)), ((), ())), preferred_element_type=F32)


def _lane_iota():
    return lax.broadcasted_iota(jnp.int32, (1, LANES), 1)


def _rope128(y, cos_t, sin_t):
    return y * cos_t + pltpu.roll(y, LANES - HALF, 1) * sin_t


def _project_body(prompt, x_ref, ng_ref, wm_ref, gq_ref, gk_ref, gcq_ref, wuq_ref, gmq_ref, gckv_ref,
                  wuk_ref, wuv_ref, gmk_ref, halves_ref, cos_ref, sin_ref, *out_refs):
    if prompt:
        qd_ref, kd_ref, kdb_ref, vd_ref, vdt_ref, c_ref, kr_ref, qm_ref, km_ref, vmt_ref = out_refs
    else:
        qd_ref, kd_ref, vd_ref, c_ref, kr_ref, qm_ref = out_refs
    x = x_ref[...]
    h = x * lax.rsqrt(jnp.mean(x * x, axis=-1, keepdims=True) + EPS) * ng_ref[...]
    hb = h.astype(BF16)
    live = (_lane_iota() < MLA_QK_DIM).astype(F32)
    cos_t = cos_ref[...]
    sin_t = sin_ref[...]
    halves = halves_ref[...]

    def lane_blocks(a):
        return [a[:, n * LANES:(n + 1) * LANES] for n in range(a.shape[1] // LANES)]

    def pair_norm(blks, g):
        sqs = [b * b for b in blks]
        his = [s.astype(BF16) for s in sqs]
        los = [(s - h_.astype(F32)).astype(BF16) for s, h_ in zip(sqs, his)]
        sums = [_dot(h_, halves) + _dot(l_, halves) for h_, l_ in zip(his, los)]
        return [b * lax.rsqrt(s / DA_HEAD_DIM + EPS) * g for b, s in zip(blks, sums)]

    def head_norm_rope(blks, g):
        sums = [jnp.sum(b * b * live, axis=-1, keepdims=True) for b in blks]
        normed = [b * lax.rsqrt(s / MLA_QK_DIM + EPS) * g for b, s in zip(blks, sums)]
        return [_rope128(y, cos_t, sin_t) for y in normed]

    def put_q(ref, vals):
        if prompt:
            outs = [jnp.transpose(v).astype(BF16) for v in vals]
            for n, o in enumerate(outs):
                ref[n * LANES:(n + 1) * LANES, :] = o
        else:
            for n, v in enumerate(vals):
                ref[:, n * LANES:(n + 1) * LANES] = v.astype(BF16)

    put_q(qd_ref, pair_norm(lane_blocks(_dot(hb, wm_ref[:, _Q0:_K0])), gq_ref[...]))
    kns = pair_norm(lane_blocks(_dot(hb, wm_ref[:, _K0:_V0])), gk_ref[...])
    va = _dot(hb, wm_ref[:, _V0:_CQ0])
    n_tok = x.shape[0]
    for hk, (kn, vh) in enumerate(zip(kns, lane_blocks(va))):
        kd_ref[pl.ds(hk, n_tok, stride=DA_KV_HEADS), :] = kn
        vd_ref[pl.ds(hk, n_tok, stride=DA_KV_HEADS), :] = vh
        if prompt:
            kdb_ref[:, hk * LANES:(hk + 1) * LANES] = kn.astype(BF16)
    if prompt:
        ones_rows = jnp.where(lax.broadcasted_iota(jnp.int32, (SUM_ROWS, x.shape[0]), 0) == 0, 1.0, 0.0).astype(BF16)
        for hk in range(DA_KV_HEADS):
            r0 = hk * (DA_V_DIM + SUM_ROWS)
            vdt_ref[r0:r0 + DA_V_DIM, :] = jnp.transpose(va[:, hk * LANES:(hk + 1) * LANES]).astype(BF16)
            vdt_ref[r0 + DA_V_DIM:r0 + DA_V_DIM + SUM_ROWS, :] = ones_rows

    cq = _dot(hb, wm_ref[:, _CQ0:_CKV0])
    cq = cq * lax.rsqrt(jnp.mean(cq * cq, axis=-1, keepdims=True) + EPS) * gcq_ref[...]
    put_q(qm_ref, head_norm_rope(lane_blocks(_dot(cq.astype(BF16), wuq_ref[...])), gmq_ref[...]))

    ckv = _dot(hb, wm_ref[:, _CKV0:_KR0])
    c = ckv * lax.rsqrt(jnp.mean(ckv * ckv, axis=-1, keepdims=True) + EPS) * gckv_ref[...]
    c_ref[...] = c
    krp = _dot(hb, wm_ref[:, _KR0:_MAIN_COLS])
    kr_ref[...] = krp[:, ROPE_LO:ROPE_LO + MLA_ROPE]

    if prompt:
        cb = c.astype(BF16)
        kn = _dot(cb, wuk_ref[...])
        kms = head_norm_rope([b + krp for b in lane_blocks(kn)], gmk_ref[...])
        for hm, km in enumerate(kms):
            km_ref[:, hm * LANES:(hm + 1) * LANES] = km.astype(BF16)
        vm = _dot(cb, wuv_ref[...])
        for blk in range(MLA_WIDTH // LANES):
            pair = jnp.transpose(vm[:, blk * LANES:(blk + 1) * LANES]).astype(BF16)
            for hh in range(2):
                r0 = (2 * blk + hh) * (MLA_V_DIM + SUM_ROWS)
                vmt_ref[r0:r0 + MLA_V_DIM, :] = pair[hh * MLA_V_DIM:(hh + 1) * MLA_V_DIM, :]
                vmt_ref[r0 + MLA_V_DIM:r0 + MLA_V_DIM + SUM_ROWS, :] = ones_rows


def _project(x2d, w, cos_tab, sin_tab, prompt, batch=None):
    n = x2d.shape[0]
    tm = min(TOKEN_TILE, n)
    assert n % tm == 0 and cos_tab.shape[0] % tm == 0
    n_pos = cos_tab.shape[0] // tm
    row = lambda cols: pl.BlockSpec((tm, cols), lambda i: (i, 0))
    pos = pl.BlockSpec((tm, LANES), lambda i: (i % n_pos, 0))
    consts = [w['norm_g'], w['w_main'], w['gq'], w['gk'], w['gcq'], w['w_uq'], w['gmq'], w['gckv'],
              w['w_uk_pad'], w['w_uv'], w['gmk'], w['halves']]
    sds = jax.ShapeDtypeStruct
    kv_shape = sds((n * DA_KV_HEADS, LANES), F32)
    kv_spec = pl.BlockSpec((tm * DA_KV_HEADS, LANES), lambda i: (i, 0))
    if prompt:
        s = cos_tab.shape[0]
        assert batch * s == n
        tr = lambda rows: pl.BlockSpec((None, rows, tm), lambda i: (i // n_pos, 0, i % n_pos))
        vdt_rows = DA_KV_HEADS * (DA_V_DIM + SUM_ROWS)
        vmt_rows = MLA_HEADS * (MLA_V_DIM + SUM_ROWS)
        out_shape = [sds((batch, DA_WIDTH, s), BF16), kv_shape, sds((n, 512), BF16), kv_shape,
                     sds((batch, vdt_rows, s), BF16), sds((n, MLA_KV_RANK), F32), sds((n, MLA_ROPE), F32),
                     sds((batch, MLA_HEADS * LANES, s), BF16), sds((n, MLA_HEADS * LANES), BF16),
                     sds((batch, vmt_rows, s), BF16)]
        out_specs = [tr(DA_WIDTH), kv_spec, row(512), kv_spec, tr(vdt_rows), row(MLA_KV_RANK), row(MLA_ROPE),
                     tr(MLA_HEADS * LANES), row(MLA_HEADS * LANES), tr(vmt_rows)]
    else:
        out_shape = [sds((n, DA_WIDTH), BF16), kv_shape, kv_shape, sds((n, MLA_KV_RANK), F32),
                     sds((n, MLA_ROPE), F32), sds((n, MLA_HEADS * LANES), BF16)]
        out_specs = [row(DA_WIDTH), kv_spec, kv_spec, row(MLA_KV_RANK), row(MLA_ROPE), row(MLA_HEADS * LANES)]
    return pl.pallas_call(
        functools.partial(_project_body, prompt),
        grid=(n // tm,),
        in_specs=[row(D_MODEL)] + [_const_spec(a.shape) for a in consts] + [pos, pos],
        out_specs=out_specs,
        out_shape=out_shape,
        compiler_params=pltpu.CompilerParams(dimension_semantics=("parallel",), vmem_limit_bytes=VMEM_LIMIT_BYTES),
        name="project_prompt" if prompt else "project_sample",
    )(x2d, *consts, cos_tab, sin_tab)


def _absorb_chunks(chunks, m_ref, acc_ref):
    stats = []
    for s, _, cols in chunks:
        m_old = m_ref[:, cols]
        m_new = jnp.maximum(m_old, jnp.max(s[...], axis=0, keepdims=True))
        m_ref[:, cols] = m_new
        stats.append((m_new, jnp.exp2(m_old - m_new)))
    probs = [jnp.exp2(s[...] - m_new).astype(BF16) for (s, _, _), (m_new, _) in zip(chunks, stats)]
    for (_, vt, cols), (_, alpha), p in zip(chunks, stats, probs):
        acc_ref[:, cols] = alpha * acc_ref[:, cols] + _dot(vt, p)


def _key_tile_pipeline(i, qk, absorb, scores_ready=False, beside_last=None):
    if not scores_ready:
        qk(0, 0)

    def last(r, buf):
        if beside_last is not None:
            beside_last()
        absorb(r, buf)

    @pl.when(i == 0)
    def _():
        last(0, 0)

    @pl.when(i >= 1)
    def _():
        qk(1, 1)
        absorb(0, 0)

        @pl.when(i == 1)
        def _():
            last(1, 1)

        @pl.when(i >= 2)
        def _():
            qk(2, 0)
            absorb(1, 1)
            n_pairs = (i - 2) // 2

            def pair(p, carry):
                r = 2 + 2 * p
                qk(r + 1, 1)
                absorb(r, 0)
                qk(r + 2, 0)
                absorb(r + 1, 1)
                return carry
            lax.fori_loop(0, n_pairs, pair, 0)
            r = 2 + 2 * n_pairs

            @pl.when(r == i)
            def _():
                last(r, 0)

            @pl.when(r != i)
            def _():
                qk(r + 1, 1)
                absorb(r, 0)
                last(r + 1, 1)


def _diff_prompt_body(lam_ref, qt_ref, k_ref, vt_ref, bdiag_ref, bsub_ref, o_ref, *scratch):
    qs_refs, s_refs, m_refs, acc_refs = scratch[0:2], scratch[2:6], scratch[6:8], scratch[8:10]
    i = pl.program_id(2)
    t = ATT_TILE
    ch = Q_CHUNK
    per_g = t // ch
    n_chunks = DA_GROUP * per_g

    row = lax.broadcasted_iota(jnp.int32, (LANES, 1), 0)
    for g in range(DA_GROUP):
        qg = qt_ref[g * LANES:(g + 1) * LANES, :]
        qs_refs[0][:, g * t:(g + 1) * t] = jnp.where(row < DA_HEAD_DIM, qg, jnp.zeros_like(qg))
        qs_refs[1][:, g * t:(g + 1) * t] = jnp.where(row < DA_HEAD_DIM, jnp.zeros_like(qg), qg)
    for mp in range(2):
        m_refs[mp][...] = jnp.full(m_refs[mp].shape, MASKED, F32)
        acc_refs[mp][...] = jnp.zeros(acc_refs[mp].shape, F32)

    def n_keys(r, cc):
        return (cc + 1) * ch if isinstance(r, int) and r == 0 else t

    def stream(mp):
        def qk(r, buf):
            ks = pl.multiple_of((i - r) * t, t)
            for c in range(n_chunks):
                g, cc = divmod(c, per_g)
                nk = n_keys(r, cc)
                cols = slice(c * ch, (c + 1) * ch)
                s = _dot(k_ref[pl.ds(ks, nk), :], qs_refs[mp][:, cols])
                if isinstance(r, int) and r == 0:
                    s = s + bdiag_ref[g, 0:nk, cc * ch:(cc + 1) * ch]
                elif isinstance(r, int) and r == 1 and cc == 0:
                    s = s + bsub_ref[g]
                s_refs[2 * mp + buf][0:nk, cols] = s

        def absorb(r, buf):
            ks = pl.multiple_of((i - r) * t, t)
            chunks = []
            for c in range(n_chunks):
                nk = n_keys(r, c % per_g)
                cols = slice(c * ch, (c + 1) * ch)
                chunks.append((s_refs[2 * mp + buf].at[0:nk, cols], vt_ref[:, pl.ds(ks, nk)], cols))
            _absorb_chunks(chunks, m_refs[mp], acc_refs[mp])
        return qk, absorb

    qk0, absorb0 = stream(0)
    qk1, absorb1 = stream(1)
    _key_tile_pipeline(i, qk0, absorb0, beside_last=lambda: qk1(0, 0))
    _key_tile_pipeline(i, qk1, absorb1, scores_ready=True)

    o0, o1 = [a[0:DA_V_DIM, :] * (1.0 / a[DA_V_DIM:DA_V_DIM + 1, :]) for a in acc_refs]
    comb = o0 - lam_ref[0] * o1
    for g in range(DA_GROUP):
        o_ref[:, g * LANES:(g + 1) * LANES] = jnp.transpose(comb[:, g * t:(g + 1) * t])


def _diff_prompt(lam, qdt, kdb, vdt, bias_diag, bias_sub):
    b, _, s = qdt.shape
    t = ATT_TILE
    assert s % t == 0
    vrows = DA_V_DIM + SUM_ROWS
    return pl.pallas_call(
        _diff_prompt_body,
        grid=(b, DA_KV_HEADS, s // t),
        in_specs=[pl.BlockSpec(memory_space=pltpu.SMEM),
                  pl.BlockSpec((None, DA_GROUP * LANES, t), lambda bi, h, i: (bi, h, i)),
                  pl.BlockSpec((None, s, LANES), lambda bi, h, i: (bi, 0, h)),
                  pl.BlockSpec((None, vrows, s), lambda bi, h, i: (bi, h, 0)),
                  pl.BlockSpec((DA_GROUP, t, t), lambda bi, h, i: (h, 0, 0)),
                  pl.BlockSpec((DA_GROUP, t, Q_CHUNK), lambda bi, h, i: (h, 0, 0))],
        out_specs=pl.BlockSpec((None, t, DA_GROUP * LANES), lambda bi, h, i: (bi, i, h)),
        out_shape=jax.ShapeDtypeStruct((b, s, DA_WIDTH), F32),
        scratch_shapes=([pltpu.VMEM((LANES, DA_GROUP * t), BF16)] * 2
                        + [pltpu.VMEM((t, DA_GROUP * t), F32)] * 4
                        + [pltpu.VMEM((1, DA_GROUP * t), F32)] * 2
                        + [pltpu.VMEM((vrows, DA_GROUP * t), F32)] * 2),
        compiler_params=pltpu.CompilerParams(
            dimension_semantics=("parallel", "parallel", "arbitrary"), vmem_limit_bytes=VMEM_LIMIT_BYTES),
        name="diff_prompt",
    )(lam, qdt, kdb, vdt, bias_diag, bias_sub)


def _mla_prompt_body(qt_ref, k_ref, vt_ref, o_ref, *scratch):
    s_refs, m_refs, acc_refs = scratch[0:4], scratch[4:6], scratch[6:8]
    i = pl.program_id(2)
    t = ATT_TILE
    ch = Q_CHUNK
    n_chunks = t // ch
    vrows = MLA_V_DIM + SUM_ROWS
    scale2 = (MLA_QK_DIM ** -0.5) * LOG2E
    for hh in range(2):
        m_refs[hh][...] = jnp.full(m_refs[hh].shape, MASKED, F32)
        acc_refs[hh][...] = jnp.zeros(acc_refs[hh].shape, F32)

    def n_keys(r, c):
        return (c + 1) * ch if isinstance(r, int) and r == 0 else t

    def stream(hh):
        hs = slice(hh * LANES, (hh + 1) * LANES)
        vs = slice(hh * vrows, (hh + 1) * vrows)

        def qk(r, buf):
            ks = pl.multiple_of((i - r) * t, t)
            for c in range(n_chunks):
                nk = n_keys(r, c)
                cols = slice(c * ch, (c + 1) * ch)
                s = _dot(k_ref[pl.ds(ks, nk), hs], qt_ref[hs, cols]) * scale2
                if isinstance(r, int) and r == 0:
                    kidx = lax.broadcasted_iota(jnp.int32, (nk, ch), 0)
                    qidx = lax.broadcasted_iota(jnp.int32, (nk, ch), 1) + c * ch
                    s = jnp.where(kidx <= qidx, s, MASKED)
                s_refs[2 * hh + buf][0:nk, cols] = s

        def absorb(r, buf):
            ks = pl.multiple_of((i - r) * t, t)
            chunks = []
            for c in range(n_chunks):
                nk = n_keys(r, c)
                cols = slice(c * ch, (c + 1) * ch)
                chunks.append((s_refs[2 * hh + buf].at[0:nk, cols], vt_ref[vs, pl.ds(ks, nk)], cols))
            _absorb_chunks(chunks, m_refs[hh], acc_refs[hh])
        return qk, absorb

    qk0, absorb0 = stream(0)
    qk1, absorb1 = stream(1)
    _key_tile_pipeline(i, qk0, absorb0, beside_last=lambda: qk1(0, 0))
    _key_tile_pipeline(i, qk1, absorb1, scores_ready=True)

    outs = [a[0:MLA_V_DIM, :] * (1.0 / a[MLA_V_DIM:MLA_V_DIM + 1, :]) for a in acc_refs]
    o_ref[...] = jnp.transpose(jnp.concatenate(outs, axis=0))


def _mla_prompt(qmt, km, vmt):
    b, _, s = qmt.shape
    t = ATT_TILE
    vrows = MLA_V_DIM + SUM_ROWS
    return pl.pallas_call(
        _mla_prompt_body,
        grid=(b, MLA_HEADS // 2, s // t),
        in_specs=[pl.BlockSpec((None, 2 * LANES, t), lambda bi, hp, i: (bi, hp, i)),
                  pl.BlockSpec((None, s, 2 * LANES), lambda bi, hp, i: (bi, 0, hp)),
                  pl.BlockSpec((None, 2 * vrows, s), lambda bi, hp, i: (bi, hp, 0))],
        out_specs=pl.BlockSpec((None, t, 2 * MLA_V_DIM), lambda bi, hp, i: (bi, i, hp)),
        out_shape=jax.ShapeDtypeStruct((b, s, MLA_WIDTH), F32),
        scratch_shapes=([pltpu.VMEM((t, t), F32)] * 4
                        + [pltpu.VMEM((1, t), F32)] * 2
                        + [pltpu.VMEM((vrows, t), F32)] * 2),
        compiler_params=pltpu.CompilerParams(
            dimension_semantics=("parallel", "parallel", "arbitrary"), vmem_limit_bytes=VMEM_LIMIT_BYTES),
        name="mla_prompt",
    )(qmt, km, vmt)


def _sample_body(n_seq, n_steps, pt_ref, lam_ref, ck_hbm, cv_hbm, cc_hbm, cr_hbm,
                 kn_ref, vn_ref, cn_ref, rn_ref, qbd_ref, qmn_ref, qrr_ref, gkn_ref, wuk_ref, wuv_ref, eseg_ref,
                 bias_ref, mmask_ref, ctab_ref, stab_ref,
                 oa_ref, ob_ref,
                 qabs_ref, md_ref, ld_ref, accd_ref, mm_ref, lm_ref, accm_ref, kbuf, vbuf, cbuf, rbuf, sems):
    pg = PAGES_PER_STEP
    b = pl.program_id(0)
    j = pl.program_id(1)
    scale_m = (MLA_QK_DIM ** -0.5) * LOG2E
    page = cc_hbm.shape[1]
    krows = page * DA_KV_HEADS
    t_new = cn_ref.shape[0]
    n_mla = MLA_HEADS * t_new

    def page_copies(pid, p, slot):
        return (pltpu.make_async_copy(ck_hbm.at[pid], kbuf.at[slot, pl.ds(p * krows, krows), :], sems.at[slot, 0]),
                pltpu.make_async_copy(cv_hbm.at[pid], vbuf.at[slot, pl.ds(p * krows, krows), :], sems.at[slot, 1]),
                pltpu.make_async_copy(cc_hbm.at[pid], cbuf.at[slot, pl.ds(p * page, page), :], sems.at[slot, 2]),
                pltpu.make_async_copy(cr_hbm.at[pid], rbuf.at[slot, :, pl.ds(p * page, page)], sems.at[slot, 3]))

    def start_step(bb, jj, slot):
        for p in range(pg):
            for cp in page_copies(pt_ref[bb, jj * pg + p], p, slot):
                cp.start()

    def wait_step(slot):
        for p in range(pg):
            for cp in page_copies(0, p, slot):
                cp.wait()

    g = b * n_steps + j
    n_global = n_seq * n_steps
    slot = lax.rem(g, PAGE_SLOTS)
    ahead = PAGE_SLOTS - 1

    @pl.when(g == 0)
    def _():
        for s0 in range(min(ahead, n_global)):
            start_step(s0 // n_steps, s0 % n_steps, s0 % PAGE_SLOTS)

    @pl.when(g + ahead < n_global)
    def _():
        nxt = g + ahead
        start_step(nxt // n_steps, lax.rem(nxt, n_steps), lax.rem(nxt, PAGE_SLOTS))

    wait_step(slot)

    def heads_to_lanes(ref2, start=0, t=None):
        t = ref2.shape[0] // DA_KV_HEADS if t is None else t
        return jnp.concatenate([ref2[pl.ds(start + h, t, stride=DA_KV_HEADS), :].astype(BF16)
                                for h in range(DA_KV_HEADS)], axis=1)

    def rows_from_lanes(alpha, n_rows):
        full = jnp.broadcast_to(alpha, (LANES, alpha.shape[1]))
        if alpha.shape[1] < LANES:
            full = jnp.concatenate([full, jnp.zeros((LANES, LANES - alpha.shape[1]), F32)], axis=1)
        return jnp.transpose(full)[:n_rows, :]

    def score_matmuls(kb, c, krt, ctab, stab):
        cb = c.astype(BF16)
        s = _dot(kb, qbd_ref[...])
        kn = _dot(cb, wuk_ref[...])
        raw = _dot(cb, qabs_ref[...])
        ext = _dot_tn(jnp.concatenate([krt * ctab, krt * stab, krt * krt], axis=0).astype(BF16), qrr_ref[...])
        return cb, s, kn, raw, ext

    def local_softmaxes(blocks):
        half = MLA_HEADS * NOPE_HALF
        ss = []
        for (cb, s, kn, raw, ext), _, _, _ in blocks:
            kn2 = kn * kn
            ss.append(_dot((kn2[:, :half] + kn2[:, half:]).astype(BF16), eseg_ref[...]))
        sa = [s + bias for (_, s, _, _, _), _, bias, _ in blocks]
        ma = [jnp.max(s, axis=0, keepdims=True) for s in sa]
        pa = [jnp.exp2(s - m) for s, m in zip(sa, ma)]
        sb = []
        for ((_, _, _, raw, ext), _, _, mmask), q in zip(blocks, ss):
            r = lax.rsqrt((q + pltpu.roll(ext, LANES - n_mla, 1)[:, :n_mla]) / MLA_QK_DIM + EPS)
            sm = (raw + ext[:, :n_mla]) * r * scale_m
            sb.append(sm if mmask is None else sm + mmask)
        mb = [jnp.max(s, axis=0, keepdims=True) for s in sb]
        pb_ = [jnp.exp2(s - m) for s, m in zip(sb, mb)]
        parts_a = [(m, jnp.sum(p, axis=0, keepdims=True), _dot_tn(p.astype(BF16), vb))
                   for m, p, (_, vb, _, _) in zip(ma, pa, blocks)]
        parts_b = [(m, jnp.sum(p, axis=0, keepdims=True), _dot_tn(p.astype(BF16), pre[0]))
                   for m, p, (pre, _, _, _) in zip(mb, pb_, blocks)]
        return parts_a, parts_b

    def fold(m_ref, l_ref, acc_ref, parts):
        n_rows, width = acc_ref.shape
        m_old = m_ref[...]
        m_new = m_old
        for m, _, _ in parts:
            m_new = jnp.maximum(m_new, m)
        wide = lambda f: jnp.concatenate([rows_from_lanes(f, n_rows)] * (width // LANES), axis=1)
        alpha = jnp.exp2(m_old - m_new)
        l_new = alpha * l_ref[...]
        acc = wide(alpha) * acc_ref[...]
        for m, l, a in parts:
            f = jnp.exp2(m - m_new)
            l_new = l_new + f * l
            acc = acc + wide(f) * a
        m_ref[...] = m_new
        l_ref[...] = l_new
        acc_ref[...] = acc

    @pl.when(j == 0)
    def _():
        qg = (qmn_ref[...].astype(F32) * gkn_ref[...]).astype(BF16)
        qabs_ref[...] = _dot(wuk_ref[...], qg).astype(BF16)
        past = n_steps * pg * page
        rows_n = pl.ds(past, t_new)
        pre = score_matmuls(heads_to_lanes(kn_ref), cn_ref[...], rn_ref[...], ctab_ref[:, rows_n], stab_ref[:, rows_n])
        ((m_d, l_d, pv),), ((m_m, l_m, pc),) = local_softmaxes(
            [(pre, heads_to_lanes(vn_ref), bias_ref[rows_n, :], mmask_ref[...])])
        md_ref[...] = m_d
        ld_ref[...] = l_d
        accd_ref[...] = pv
        mm_ref[...] = m_m
        lm_ref[...] = l_m
        accm_ref[...] = pc

    base = j * (pg * page)
    pb = PAGES_PER_BLOCK
    blocks = []
    tb = pb * page
    for blk in range(pg // pb):
        rows = pl.ds(pl.multiple_of(base + blk * tb, tb), tb)
        keys = slice(blk * tb, (blk + 1) * tb)
        pre = score_matmuls(heads_to_lanes(kbuf.at[slot], blk * tb * DA_KV_HEADS, tb), cbuf[slot, keys, :],
                            rbuf[slot, :, keys], ctab_ref[:, rows], stab_ref[:, rows])
        blocks.append((pre, heads_to_lanes(vbuf.at[slot], blk * tb * DA_KV_HEADS, tb), bias_ref[rows, :], None))
    parts_d, parts_m = local_softmaxes(blocks)
    fold(md_ref, ld_ref, accd_ref, parts_d)
    fold(mm_ref, lm_ref, accm_ref, parts_m)

    @pl.when(j == n_steps - 1)
    def _():
        lam = lam_ref[0]
        inv_d = rows_from_lanes(1.0 / ld_ref[...], LANES)
        accd = accd_ref[...]
        lanes_per_kv = 2 * DA_GROUP * t_new
        for h in range(DA_KV_HEADS):
            cols = slice(h * LANES, (h + 1) * LANES)
            for g in range(DA_GROUP):
                r1 = h * lanes_per_kv + g * t_new
                r2 = r1 + DA_GROUP * t_new
                o1 = accd[r1:r1 + t_new, cols] * inv_d[r1:r1 + t_new, :]
                o2 = accd[r2:r2 + t_new, cols] * inv_d[r2:r2 + t_new, :]
                hq = h * DA_GROUP + g
                oa_ref[:, hq * LANES:(hq + 1) * LANES] = o1 - lam * o2
        inv_m = rows_from_lanes(1.0 / lm_ref[...], n_mla)
        cbar = (accm_ref[...] * jnp.concatenate([inv_m] * (MLA_KV_RANK // LANES), axis=1)).astype(BF16)
        full = _dot(cbar, wuv_ref[...])
        lane5 = lax.broadcasted_iota(jnp.int32, (1, MLA_WIDTH), 1)
        ob = jnp.zeros((t_new, MLA_WIDTH), F32)
        for h in range(MLA_HEADS):
            sel = (lane5 >= h * MLA_V_DIM) & (lane5 < (h + 1) * MLA_V_DIM)
            ob = ob + jnp.where(sel, full[h * t_new:(h + 1) * t_new, :], 0.0)
        ob_ref[...] = ob


def _sample_attention(lam, page_table, ck, cv, cc, crt, new, qbd, qmn, qrr, w, bias_tab, mla_mask, ctab, stab):
    n_seq, n_pages = page_table.shape
    pool, page = cc.shape[0], cc.shape[1]
    t_new = new['c'].shape[1]
    pg = PAGES_PER_STEP
    assert n_pages % pg == 0 and pg % PAGES_PER_BLOCK == 0 and MLA_HEADS * t_new <= LANES // 2
    n_steps = n_pages // pg

    def seq_spec(*tail):
        zeros = (0,) * len(tail)
        return pl.BlockSpec((None,) + tail, lambda b, j, pt: (b,) + zeros)

    def const(shape):
        nd = len(shape)
        return pl.BlockSpec(shape, lambda b, j, pt: (0,) * nd, pipeline_mode=pl.Buffered(1))

    in_specs = [pl.BlockSpec(memory_space=pltpu.SMEM)] + [pl.BlockSpec(memory_space=pl.ANY)] * 4
    per_seq = [new['kd'], new['vd'], new['c'], new['krt'], qbd, qmn, qrr]
    in_specs += [seq_spec(*a.shape[1:]) for a in per_seq]
    consts = [w['gk_nope_col'], w['w_uk'], w['w_uv'], w['eseg'], bias_tab, mla_mask, ctab, stab]
    in_specs += [const(a.shape) for a in consts]
    operands = [ck, cv, cc, crt] + per_seq + consts
    krows = page * DA_KV_HEADS
    grid_spec = pltpu.PrefetchScalarGridSpec(
        num_scalar_prefetch=1,
        grid=(n_seq, n_steps),
        in_specs=in_specs,
        out_specs=[pl.BlockSpec((None, t_new, DA_WIDTH), lambda b, j, pt: (b, 0, 0)),
                   pl.BlockSpec((None, t_new, MLA_WIDTH), lambda b, j, pt: (b, 0, 0))],
        scratch_shapes=[pltpu.VMEM((MLA_KV_RANK, MLA_HEADS * t_new), BF16),
                        pltpu.VMEM((1, LANES), F32), pltpu.VMEM((1, LANES), F32),
                        pltpu.VMEM((LANES, DA_KV_HEADS * DA_V_DIM), F32),
                        pltpu.VMEM((1, MLA_HEADS * t_new), F32), pltpu.VMEM((1, MLA_HEADS * t_new), F32),
                        pltpu.VMEM((MLA_HEADS * t_new, MLA_KV_RANK), F32),
                        pltpu.VMEM((PAGE_SLOTS, pg * krows, LANES), F32),
                        pltpu.VMEM((PAGE_SLOTS, pg * krows, DA_V_DIM), F32),
                        pltpu.VMEM((PAGE_SLOTS, pg * page, MLA_KV_RANK), F32),
                        pltpu.VMEM((PAGE_SLOTS, MLA_ROPE, pg * page), F32),
                        pltpu.SemaphoreType.DMA((PAGE_SLOTS, 4))])
    return pl.pallas_call(
        functools.partial(_sample_body, n_seq, n_steps),
        grid_spec=grid_spec,
        out_shape=[jax.ShapeDtypeStruct((n_seq, t_new, DA_WIDTH), F32),
                   jax.ShapeDtypeStruct((n_seq, t_new, MLA_WIDTH), F32)],
        compiler_params=pltpu.CompilerParams(dimension_semantics=("arbitrary", "arbitrary"),
                                             vmem_limit_bytes=VMEM_LIMIT_BYTES),
        name="sample_attention",
    )(page_table, lam, *operands)


def _merge_body(one_minus_lam_init, x_ref, oa_ref, ob_ref, ng_ref, wzg_ref, gout_ref, wpa_ref, wpb_ref, wo_ref, y_ref):
    x = x_ref[...]
    h = x * lax.rsqrt(jnp.mean(x * x, axis=-1, keepdims=True) + EPS) * ng_ref[...]
    hb = h.astype(BF16)
    za = _dot(hb, wzg_ref[:, 0:DA_WIDTH])
    gout = gout_ref[...]
    parts = []
    for hq in range(DA_HEADS):
        sl = slice(hq * LANES, (hq + 1) * LANES)
        o = oa_ref[:, sl]
        o = o * lax.rsqrt(jnp.mean(o * o, axis=-1, keepdims=True) + EPS) * gout * one_minus_lam_init
        z = za[:, sl]
        parts.append((o * (z * jax.nn.sigmoid(z))).astype(BF16))
    ya = _dot(jnp.concatenate(parts, axis=1), wpa_ref[...])
    zb = _dot(hb, wzg_ref[:, DA_WIDTH:DA_WIDTH + MLA_WIDTH])
    yb = _dot((ob_ref[...] * (zb * jax.nn.sigmoid(zb))).astype(BF16), wpb_ref[...])
    g0 = DA_WIDTH + MLA_WIDTH
    ga = _dot(hb, wzg_ref[:, g0:g0 + D_MODEL])
    gb = _dot(hb, wzg_ref[:, g0 + D_MODEL:g0 + 2 * D_MODEL])
    mix = jax.nn.sigmoid(ga) * ya + jax.nn.sigmoid(gb) * yb
    y_ref[...] = x + _dot(mix.astype(BF16), wo_ref[...])


def _merge(x2d, oa, ob, w, lam_init):
    n = x2d.shape[0]
    tm = min(TOKEN_TILE, n)
    assert n % tm == 0
    row = lambda cols: pl.BlockSpec((tm, cols), lambda i: (i, 0))
    consts = [w['norm_g'], w['w_zg'], w['gout'], w['w_pa'], w['w_pb'], w['w_o']]
    return pl.pallas_call(
        functools.partial(_merge_body, 1.0 - lam_init),
        grid=(n // tm,),
        in_specs=[row(D_MODEL), row(DA_WIDTH), row(MLA_WIDTH)] + [_const_spec(a.shape) for a in consts],
        out_specs=row(D_MODEL),
        out_shape=jax.ShapeDtypeStruct((n, D_MODEL), F32),
        compiler_params=pltpu.CompilerParams(dimension_semantics=("parallel",), vmem_limit_bytes=VMEM_LIMIT_BYTES),
        name="merge",
    )(x2d, oa, ob, *consts)


def _rel_bucket(dist):
    n = jnp.maximum(dist, 0)
    max_exact = REL_BUCKETS // 2
    nf = jnp.maximum(n, 1).astype(F32)
    large = max_exact + (jnp.log(nf / max_exact) / math.log(REL_MAX_DIST / max_exact)
                         * (REL_BUCKETS - max_exact)).astype(jnp.int32)
    large = jnp.minimum(large, REL_BUCKETS - 1)
    return jnp.where(n < max_exact, n, large)


def _rope_angles(pos):
    inv = jnp.power(ROPE_THETA, -jnp.arange(HALF, dtype=F32) / HALF)
    ang = pos.astype(F32)[:, None] * inv[None, :]
    return jnp.cos(ang), jnp.sin(ang)


def _rope_tables128(pos):
    cos, sin = _rope_angles(pos)
    n = pos.shape[0]
    cos_t = jnp.concatenate([jnp.ones((n, MLA_NOPE), F32), cos, cos, jnp.zeros((n, LANES - MLA_QK_DIM), F32)], axis=1)
    sin_t = jnp.concatenate([jnp.zeros((n, MLA_NOPE), F32), -sin, sin, jnp.zeros((n, LANES - MLA_QK_DIM), F32)], axis=1)
    return cos_t, sin_t


def _pad_heads(a, live):
    pad = [(0, 0)] * (a.ndim - 1) + [(0, LANES - live)]
    a = jnp.pad(a, pad)
    return a.reshape(a.shape[:-2] + (a.shape[-2] * LANES,))


def _prep_layer(p, t_new):
    offs = [int(v) for v in np.cumsum((0,) + IN_SPLITS)]
    wq, wk, wv, wza, wcq, wckv, wkr, wzb, wga, wgb = [p['w_in'][:, offs[n]:offs[n + 1]] for n in range(10)]
    with_x1_copy = lambda a: jnp.concatenate([a, a[..., MLA_NOPE:MLA_NOPE + HALF]], axis=-1)
    tail = LANES - MLA_QK_DIM - HALF
    wkr_placed = jnp.pad(jnp.concatenate([wkr, wkr[:, :HALF]], axis=1), ((0, 0), (ROPE_LO, tail)))
    halves = np.arange(LANES) // DA_HEAD_DIM
    w_ukv = p['w_ukv'].reshape(MLA_KV_RANK, MLA_HEADS, MLA_NOPE + MLA_V_DIM)
    w_uk = w_ukv[:, :, :MLA_NOPE]
    w_uk_halves = jnp.transpose(w_uk.reshape(MLA_KV_RANK, MLA_HEADS, 2, NOPE_HALF), (0, 2, 1, 3))
    gmk = p['mla_k_norm']
    gk_nope = jnp.broadcast_to(gmk[:MLA_NOPE].reshape(2, 1, NOPE_HALF), (2, MLA_HEADS, NOPE_HALF))
    head_of_row = np.arange(MLA_HEADS * NOPE_HALF) // NOPE_HALF
    head_of_lane = np.arange(MLA_HEADS * t_new) // t_new
    return dict(
        norm_g=p['norm_g'].reshape(1, D_MODEL),
        w_main=jnp.concatenate([wq, wk, wv, wcq, wckv, wkr_placed], axis=1).astype(BF16),
        w_zg=jnp.concatenate([wza, wzb, wga, wgb], axis=1).astype(BF16),
        gq=p['da_q_norm'].reshape(1, LANES) * ((DA_HEAD_DIM ** -0.5) * LOG2E),
        gk=p['da_k_norm'].reshape(1, LANES),
        gcq=p['mla_cq_norm'].reshape(1, MLA_Q_RANK),
        w_uq=_pad_heads(with_x1_copy(p['w_uq'].reshape(MLA_Q_RANK, MLA_HEADS, MLA_QK_DIM)),
                        MLA_QK_DIM + HALF).astype(BF16),
        gmq=jnp.pad(with_x1_copy(p['mla_q_norm']), (0, tail)).reshape(1, LANES),
        halves=jnp.asarray(halves[:, None] == halves[None, :], BF16),
        gckv=p['mla_ckv_norm'].reshape(1, MLA_KV_RANK),
        w_uk_pad=_pad_heads(w_uk, MLA_NOPE).astype(BF16),
        w_uk=w_uk_halves.reshape(MLA_KV_RANK, MLA_HEADS * MLA_NOPE).astype(BF16),
        w_uv=w_ukv[:, :, MLA_NOPE:].reshape(MLA_KV_RANK, MLA_WIDTH).astype(BF16),
        gmk=jnp.pad(with_x1_copy(gmk), (0, tail)).reshape(1, LANES),
        gk_nope_col=gk_nope.reshape(MLA_HEADS * MLA_NOPE, 1),
        gk_rope=gmk[MLA_NOPE:],
        eseg=jnp.asarray(head_of_row[:, None] == head_of_lane[None, :], BF16),
        gout=p['da_out_norm'].reshape(1, DA_V_DIM),
        w_pa=p['w_pa'].astype(BF16), w_pb=p['w_pb'].astype(BF16), w_o=p['w_o'].astype(BF16),
    )


def _toeplitz(v, n):
    lead = v.shape[:-1]
    u = jnp.flip(v, axis=-1)
    w = jnp.concatenate([u, jnp.zeros(lead + (1,), v.dtype)], axis=-1)
    flat = jnp.tile(w, (1,) * len(lead) + (n,))[..., :n * (2 * n - 1)]
    return flat.reshape(lead + (n, 2 * n - 1))[..., n - 1:]


def _prompt_bias_tiles(rel_bias):
    t = ATT_TILE
    assert REL_MAX_DIST == LANES
    far = rel_bias[REL_BUCKETS - 1]
    d0 = jnp.arange(2 * LANES - 1, dtype=jnp.int32) - (LANES - 1)
    f0 = jnp.where((d0 >= 0)[:, None], rel_bias[_rel_bucket(d0)] - far[None, :], MASKED)
    on_diag = jnp.swapaxes(_toeplitz(f0.T, LANES), 1, 2) * LOG2E
    below = jnp.swapaxes(_toeplitz((rel_bias[_rel_bucket(d0 + LANES)] - far[None, :]).T, LANES), 1, 2) * LOG2E
    masked = jnp.full_like(on_diag, MASKED)
    zero = jnp.zeros_like(on_diag)
    nb = t // LANES
    diag = jnp.concatenate(
        [jnp.concatenate([masked if qb < kb else on_diag if qb == kb else below if qb == kb + 1 else zero
                          for qb in range(nb)], axis=2) for kb in range(nb)], axis=1)
    sub = jnp.pad(below, ((0, 0), (t - LANES, 0), (0, Q_CHUNK - LANES)))
    return diag, sub


def _sample_tables(rel_bias, gk_rope, past, t_new):
    n_keys = past + t_new
    near = REL_MAX_DIST + t_new
    k_pos = jnp.arange(n_keys - near, n_keys, dtype=jnp.int32)
    q_pos = past + jnp.arange(t_new, dtype=jnp.int32)
    b = rel_bias[_rel_bucket(q_pos[None, :] - k_pos[:, None])]
    b = b - rel_bias[REL_BUCKETS - 1][None, None, :]
    b = jnp.where((q_pos[None, :] >= k_pos[:, None])[:, :, None], b, MASKED)
    b = jnp.transpose(b.reshape(near, t_new, DA_KV_HEADS, DA_GROUP), (0, 2, 3, 1))
    b = jnp.broadcast_to(b[:, :, None], (near, DA_KV_HEADS, 2, DA_GROUP, t_new)).reshape(near, 2 * DA_HEADS * t_new)
    b = jnp.pad(b.astype(F32) * LOG2E, ((n_keys - near, 0), (0, 0)))
    newk = jnp.arange(t_new)
    mm = jnp.where(newk[None, :] >= newk[:, None], 0.0, MASKED).astype(F32)
    mm = jnp.tile(mm, (1, MLA_HEADS))
    cos, sin = _rope_angles(jnp.arange(n_keys, dtype=jnp.int32))
    ctab = jnp.concatenate([cos, cos], axis=1) * gk_rope[None, :]
    stab = jnp.concatenate([sin, sin], axis=1) * gk_rope[None, :]
    return b, mm, ctab.T, stab.T


def _sample_query_layouts(qd, qm, n_seq, t_new):
    q = qd.reshape(n_seq, t_new, DA_KV_HEADS, DA_GROUP, 2, DA_HEAD_DIM)
    q = jnp.transpose(q, (0, 2, 4, 5, 3, 1)).reshape(n_seq, DA_KV_HEADS * 2, DA_HEAD_DIM, DA_GROUP * t_new)
    eye = jnp.eye(DA_KV_HEADS * 2, dtype=qd.dtype)
    qbd = q[:, :, :, None, :] * eye[None, :, None, :, None]
    qbd = qbd.reshape(n_seq, DA_KV_HEADS * 2 * DA_HEAD_DIM, DA_KV_HEADS * 2 * DA_GROUP * t_new)
    qh = qm.reshape(n_seq, t_new, MLA_HEADS, LANES)
    qn = jnp.transpose(qh[..., :MLA_NOPE].reshape(n_seq, t_new, MLA_HEADS, 2, NOPE_HALF), (0, 3, 2, 4, 1))
    eye_h = jnp.eye(MLA_HEADS, dtype=qm.dtype)
    qmn = (qn[:, :, :, :, None, :] * eye_h[None, None, :, None, :, None]).reshape(
        n_seq, MLA_HEADS * MLA_NOPE, MLA_HEADS * t_new)
    qr = jnp.transpose(qh[..., ROPE_LO:ROPE_LO + MLA_ROPE], (0, 3, 2, 1)).reshape(n_seq, MLA_ROPE, MLA_HEADS * t_new)
    qrs = jnp.concatenate([qr[:, HALF:], -qr[:, :HALF]], axis=1)
    n_mla = MLA_HEADS * t_new
    top = jnp.pad(jnp.concatenate([qr, qrs], axis=1), ((0, 0), (0, 0), (0, LANES - n_mla)))
    bottom = jnp.pad(jnp.ones((n_seq, MLA_ROPE, n_mla), qm.dtype), ((0, 0), (0, 0), (LANES - n_mla, 0)))
    return qbd, qmn, jnp.concatenate([top, bottom], axis=1)


def kernel(x_prompt, x_sample, cache_diff_k, cache_diff_v, cache_mla_latent, cache_mla_krope, page_table, rel_bias,
           norm_g, w_in, da_q_norm, da_k_norm, lam_q1, lam_k1, lam_q2, lam_k2, da_out_norm, w_pa, mla_cq_norm, w_uq,
           mla_ckv_norm, w_ukv, mla_q_norm, mla_k_norm, w_pb, w_o):
    b, s, _ = x_prompt.shape
    n_seq, t_new, _ = x_sample.shape
    depth, pool, page = cache_diff_k.shape[:3]
    past = page_table.shape[1] * page
    page_table = page_table.astype(jnp.int32)

    cos_p, sin_p = _rope_tables128(jnp.arange(s, dtype=jnp.int32))
    rows_s = min(TOKEN_TILE, n_seq * t_new)
    cos_s, sin_s = _rope_tables128(past + (jnp.arange(rows_s, dtype=jnp.int32) % t_new))
    bias_diag, bias_sub = _prompt_bias_tiles(rel_bias)

    xp = x_prompt.reshape(b * s, D_MODEL)
    xs = x_sample.reshape(n_seq * t_new, D_MODEL)
    news = [[] for _ in range(8)]
    for l in range(depth):
        p = dict(norm_g=norm_g[l], w_in=w_in[l], da_q_norm=da_q_norm[l], da_k_norm=da_k_norm[l],
                 da_out_norm=da_out_norm[l], w_pa=w_pa[l], mla_cq_norm=mla_cq_norm[l], w_uq=w_uq[l],
                 mla_ckv_norm=mla_ckv_norm[l], w_ukv=w_ukv[l], mla_q_norm=mla_q_norm[l],
                 mla_k_norm=mla_k_norm[l], w_pb=w_pb[l], w_o=w_o[l])
        w = _prep_layer(p, t_new)
        lam_init = 0.8 - 0.6 * math.exp(-0.3 * l)
        lam = (jnp.exp(jnp.sum(lam_q1[l] * lam_k1[l]).astype(F32))
               - jnp.exp(jnp.sum(lam_q2[l] * lam_k2[l]).astype(F32)) + lam_init).reshape(1)

        qdt, kd, kdb, vd, vdt, c, kr, qmt, km, vmt = _project(xp, w, cos_p, sin_p, True, batch=b)
        oa_p = _diff_prompt(lam, qdt, kdb.reshape(b, s, 512), vdt, bias_diag, bias_sub)
        ob_p = _mla_prompt(qmt, km.reshape(b, s, MLA_HEADS * LANES), vmt)

        qd_s, kd_s, vd_s, c_s, kr_s, qm_s = _project(xs, w, cos_s, sin_s, False)
        qbd, qmn, qrr = _sample_query_layouts(qd_s, qm_s, n_seq, t_new)
        bias_tab, mla_mask, ctab, stab = _sample_tables(rel_bias, w['gk_rope'], past, t_new)
        new = dict(kd=kd_s.reshape(n_seq, t_new * DA_KV_HEADS, LANES), vd=vd_s.reshape(n_seq, t_new * DA_KV_HEADS, DA_V_DIM),
                   c=c_s.reshape(n_seq, t_new, MLA_KV_RANK),
                   krt=jnp.swapaxes(kr_s.reshape(n_seq, t_new, MLA_ROPE), 1, 2))
        oa_s, ob_s = _sample_attention(
            lam, page_table, cache_diff_k[l].reshape(pool, page * DA_KV_HEADS, LANES),
            cache_diff_v[l].reshape(pool, page * DA_KV_HEADS, DA_V_DIM), cache_mla_latent[l],
            jnp.swapaxes(cache_mla_krope[l], 1, 2), new, qbd, qmn, qrr, w, bias_tab, mla_mask, ctab, stab)

        xp = _merge(xp, oa_p.reshape(b * s, DA_WIDTH), ob_p.reshape(b * s, MLA_WIDTH), w, lam_init)
        xs = _merge(xs, oa_s.reshape(n_seq * t_new, DA_WIDTH), ob_s.reshape(n_seq * t_new, MLA_WIDTH), w, lam_init)

        for lst, a in zip(news, (kd.reshape(b, s, DA_KV_HEADS, LANES), vd.reshape(b, s, DA_KV_HEADS, DA_V_DIM),
                                 c.reshape(b, s, MLA_KV_RANK), kr.reshape(b, s, MLA_ROPE),
                                 kd_s.reshape(n_seq, t_new, DA_KV_HEADS, LANES),
                                 vd_s.reshape(n_seq, t_new, DA_KV_HEADS, DA_V_DIM),
                                 c_s.reshape(n_seq, t_new, MLA_KV_RANK), kr_s.reshape(n_seq, t_new, MLA_ROPE))):
            lst.append(a)
    return (xp.reshape(b, s, D_MODEL), xs.reshape(n_seq, t_new, D_MODEL)) + tuple(jnp.stack(a) for a in news)
```

```python
import functools
import math

import numpy as np
import jax
import jax.numpy as jnp
from jax import lax
from jax.experimental import pallas as pl
from jax.experimental.pallas import tpu as pltpu

F32 = jnp.float32
BF16 = jnp.bfloat16

D_MODEL = 1024
DA_HEADS = 8
DA_KV_HEADS = 4
DA_GROUP = DA_HEADS // DA_KV_HEADS
DA_HEAD_DIM = 64
DA_V_DIM = 2 * DA_HEAD_DIM
DA_WIDTH = DA_HEADS * DA_V_DIM
MLA_HEADS = 8
MLA_Q_RANK = 384
MLA_KV_RANK = 256
MLA_NOPE = 64
MLA_ROPE = 32
MLA_QK_DIM = MLA_NOPE + MLA_ROPE
MLA_V_DIM = 64
MLA_WIDTH = MLA_HEADS * MLA_V_DIM
ROPE_THETA = 10000.0
REL_BUCKETS = 32
REL_MAX_DIST = 128
EPS = 1e-6
IN_SPLITS = (DA_HEADS * 2 * DA_HEAD_DIM, DA_KV_HEADS * 2 * DA_HEAD_DIM, DA_KV_HEADS * DA_V_DIM, DA_WIDTH,
             MLA_Q_RANK, MLA_KV_RANK, MLA_ROPE, MLA_WIDTH, D_MODEL, D_MODEL)

LANES = 128
VMEM_LIMIT_BYTES = 56 * 1024 * 1024

MASKED = -1e30
HALF = MLA_ROPE // 2
ROPE_LO = MLA_NOPE
NOPE_HALF = MLA_NOPE // 2
SUM_ROWS = 16
LOG2E = math.log2(math.e)

_Q0, _K0, _V0, _CQ0, _CKV0, _KR0, _MAIN_COLS = 0, 1024, 1536, 2048, 2432, 2688, 2816

TOKEN_TILE = 512
ATT_TILE = 512
Q_CHUNK = 256
PAGES_PER_STEP = 8
PAGES_PER_BLOCK = 2
PAGE_SLOTS = 3


def _const_spec(shape):
    nd = len(shape)
    return pl.BlockSpec(shape, lambda *_: (0,) * nd, pipeline_mode=pl.Buffered(1))


def _dot(a, b):
    return jnp.dot(a, b, preferred_element_type=F32)


def _dot_tn(a, b):
    return lax.dot_general(a, b, (((0,), (0,)), ((), ())), preferred_element_type=F32)


def _lane_iota():
    return lax.broadcasted_iota(jnp.int32, (1, LANES), 1)


def _rope128(y, cos_t, sin_t):
    return y * cos_t + pltpu.roll(y, LANES - HALF, 1) * sin_t


def _project_body(prompt, x_ref, ng_ref, wm_ref, gq_ref, gk_ref, gcq_ref, wuq_ref, gmq_ref, gckv_ref,
                  wuk_ref, wuv_ref, gmk_ref, halves_ref, cos_ref, sin_ref, *out_refs):
    if prompt:
        qd_ref, kd_ref, kdb_ref, vd_ref, vdt_ref, c_ref, kr_ref, qm_ref, km_ref, vmt_ref = out_refs
    else:
        qd_ref, kd_ref, vd_ref, c_ref, kr_ref, qm_ref = out_refs
    x = x_ref[...]
    h = x * lax.rsqrt(jnp.mean(x * x, axis=-1, keepdims=True) + EPS) * ng_ref[...]
    hb = h.astype(BF16)
    live = (_lane_iota() < MLA_QK_DIM).astype(F32)
    cos_t = cos_ref[...]
    sin_t = sin_ref[...]
    halves = halves_ref[...]

    def lane_blocks(a):
        return [a[:, n * LANES:(n + 1) * LANES] for n in range(a.shape[1] // LANES)]

    def pair_norm(blks, g):
        sqs = [b * b for b in blks]
        his = [s.astype(BF16) for s in sqs]
        los = [(s - h_.astype(F32)).astype(BF16) for s, h_ in zip(sqs, his)]
        sums = [_dot(h_, halves) + _dot(l_, halves) for h_, l_ in zip(his, los)]
        return [b * lax.rsqrt(s / DA_HEAD_DIM + EPS) * g for b, s in zip(blks, sums)]

    def head_norm_rope(blks, g):
        sums = [jnp.sum(b * b * live, axis=-1, keepdims=True) for b in blks]
        normed = [b * lax.rsqrt(s / MLA_QK_DIM + EPS) * g for b, s in zip(blks, sums)]
        return [_rope128(y, cos_t, sin_t) for y in normed]

    def put_q(ref, vals):
        if prompt:
            outs = [jnp.transpose(v).astype(BF16) for v in vals]
            for n, o in enumerate(outs):
                ref[n * LANES:(n + 1) * LANES, :] = o
        else:
            for n, v in enumerate(vals):
                ref[:, n * LANES:(n + 1) * LANES] = v.astype(BF16)

    put_q(qd_ref, pair_norm(lane_blocks(_dot(hb, wm_ref[:, _Q0:_K0])), gq_ref[...]))
    kns = pair_norm(lane_blocks(_dot(hb, wm_ref[:, _K0:_V0])), gk_ref[...])
    va = _dot(hb, wm_ref[:, _V0:_CQ0])
    n_tok = x.shape[0]
    for hk, (kn, vh) in enumerate(zip(kns, lane_blocks(va))):
        kd_ref[pl.ds(hk, n_tok, stride=DA_KV_HEADS), :] = kn
        vd_ref[pl.ds(hk, n_tok, stride=DA_KV_HEADS), :] = vh
        if prompt:
            kdb_ref[:, hk * LANES:(hk + 1) * LANES] = kn.astype(BF16)
    if prompt:
        ones_rows = jnp.where(lax.broadcasted_iota(jnp.int32, (SUM_ROWS, x.shape[0]), 0) == 0, 1.0, 0.0).astype(BF16)
        for hk in range(DA_KV_HEADS):
            r0 = hk * (DA_V_DIM + SUM_ROWS)
            vdt_ref[r0:r0 + DA_V_DIM, :] = jnp.transpose(va[:, hk * LANES:(hk + 1) * LANES]).astype(BF16)
            vdt_ref[r0 + DA_V_DIM:r0 + DA_V_DIM + SUM_ROWS, :] = ones_rows

    cq = _dot(hb, wm_ref[:, _CQ0:_CKV0])
    cq = cq * lax.rsqrt(jnp.mean(cq * cq, axis=-1, keepdims=True) + EPS) * gcq_ref[...]
    put_q(qm_ref, head_norm_rope(lane_blocks(_dot(cq.astype(BF16), wuq_ref[...])), gmq_ref[...]))

    ckv = _dot(hb, wm_ref[:, _CKV0:_KR0])
    c = ckv * lax.rsqrt(jnp.mean(ckv * ckv, axis=-1, keepdims=True) + EPS) * gckv_ref[...]
    c_ref[...] = c
    krp = _dot(hb, wm_ref[:, _KR0:_MAIN_COLS])
    kr_ref[...] = krp[:, ROPE_LO:ROPE_LO + MLA_ROPE]

    if prompt:
        cb = c.astype(BF16)
        kn = _dot(cb, wuk_ref[...])
        kms = head_norm_rope([b + krp for b in lane_blocks(kn)], gmk_ref[...])
        for hm, km in enumerate(kms):
            km_ref[:, hm * LANES:(hm + 1) * LANES] = km.astype(BF16)
        vm = _dot(cb, wuv_ref[...])
        for blk in range(MLA_WIDTH // LANES):
            pair = jnp.transpose(vm[:, blk * LANES:(blk + 1) * LANES]).astype(BF16)
            for hh in range(2):
                r0 = (2 * blk + hh) * (MLA_V_DIM + SUM_ROWS)
                vmt_ref[r0:r0 + MLA_V_DIM, :] = pair[hh * MLA_V_DIM:(hh + 1) * MLA_V_DIM, :]
                vmt_ref[r0 + MLA_V_DIM:r0 + MLA_V_DIM + SUM_ROWS, :] = ones_rows


def _project(x2d, w, cos_tab, sin_tab, prompt, batch=None):
    n = x2d.shape[0]
    tm = min(TOKEN_TILE, n)
    assert n % tm == 0 and cos_tab.shape[0] % tm == 0
    n_pos = cos_tab.shape[0] // tm
    row = lambda cols: pl.BlockSpec((tm, cols), lambda i: (i, 0))
    pos = pl.BlockSpec((tm, LANES), lambda i: (i % n_pos, 0))
    consts = [w['norm_g'], w['w_main'], w['gq'], w['gk'], w['gcq'], w['w_uq'], w['gmq'], w['gckv'],
              w['w_uk_pad'], w['w_uv'], w['gmk'], w['halves']]
    sds = jax.ShapeDtypeStruct
    kv_shape = sds((n * DA_KV_HEADS, LANES), F32)
    kv_spec = pl.BlockSpec((tm * DA_KV_HEADS, LANES), lambda i: (i, 0))
    if prompt:
        s = cos_tab.shape[0]
        assert batch * s == n
        tr = lambda rows: pl.BlockSpec((None, rows, tm), lambda i: (i // n_pos, 0, i % n_pos))
        vdt_rows = DA_KV_HEADS * (DA_V_DIM + SUM_ROWS)
        vmt_rows = MLA_HEADS * (MLA_V_DIM + SUM_ROWS)
        out_shape = [sds((batch, DA_WIDTH, s), BF16), kv_shape, sds((n, 512), BF16), kv_shape,
                     sds((batch, vdt_rows, s), BF16), sds((n, MLA_KV_RANK), F32), sds((n, MLA_ROPE), F32),
                     sds((batch, MLA_HEADS * LANES, s), BF16), sds((n, MLA_HEADS * LANES), BF16),
                     sds((batch, vmt_rows, s), BF16)]
        out_specs = [tr(DA_WIDTH), kv_spec, row(512), kv_spec, tr(vdt_rows), row(MLA_KV_RANK), row(MLA_ROPE),
                     tr(MLA_HEADS * LANES), row(MLA_HEADS * LANES), tr(vmt_rows)]
    else:
        out_shape = [sds((n, DA_WIDTH), BF16), kv_shape, kv_shape, sds((n, MLA_KV_RANK), F32),
                     sds((n, MLA_ROPE), F32), sds((n, MLA_HEADS * LANES), BF16)]
        out_specs = [row(DA_WIDTH), kv_spec, kv_spec, row(MLA_KV_RANK), row(MLA_ROPE), row(MLA_HEADS * LANES)]
    return pl.pallas_call(
        functools.partial(_project_body, prompt),
        grid=(n // tm,),
        in_specs=[row(D_MODEL)] + [_const_spec(a.shape) for a in consts] + [pos, pos],
        out_specs=out_specs,
        out_shape=out_shape,
        compiler_params=pltpu.CompilerParams(dimension_semantics=("parallel",), vmem_limit_bytes=VMEM_LIMIT_BYTES),
        name="project_prompt" if prompt else "project_sample",
    )(x2d, *consts, cos_tab, sin_tab)


def _absorb_chunks(chunks, m_ref, acc_ref):
    stats = []
    for s, _, cols in chunks:
        m_old = m_ref[:, cols]
        m_new = jnp.maximum(m_old, jnp.max(s[...], axis=0, keepdims=True))
        m_ref[:, cols] = m_new
        stats.append((m_new, jnp.exp2(m_old - m_new)))
    probs = [jnp.exp2(s[...] - m_new).astype(BF16) for (s, _, _), (m_new, _) in zip(chunks, stats)]
    for (_, vt, cols), (_, alpha), p in zip(chunks, stats, probs):
        acc_ref[:, cols] = alpha * acc_ref[:, cols] + _dot(vt, p)


def _key_tile_pipeline(i, qk, absorb, scores_ready=False, beside_last=None):
    if not scores_ready:
        qk(0, 0)

    def last(r, buf):
        if beside_last is not None:
            beside_last()
        absorb(r, buf)

    @pl.when(i == 0)
    def _():
        last(0, 0)

    @pl.when(i >= 1)
    def _():
        qk(1, 1)
        absorb(0, 0)

        @pl.when(i == 1)
        def _():
            last(1, 1)

        @pl.when(i >= 2)
        def _():
            qk(2, 0)
            absorb(1, 1)
            n_pairs = (i - 2) // 2

            def pair(p, carry):
                r = 2 + 2 * p
                qk(r + 1, 1)
                absorb(r, 0)
                qk(r + 2, 0)
                absorb(r + 1, 1)
                return carry
            lax.fori_loop(0, n_pairs, pair, 0)
            r = 2 + 2 * n_pairs

            @pl.when(r == i)
            def _():
                last(r, 0)

            @pl.when(r != i)
            def _():
                qk(r + 1, 1)
                absorb(r, 0)
                last(r + 1, 1)


def _diff_prompt_body(lam_ref, qt_ref, k_ref, vt_ref, bdiag_ref, bsub_ref, o_ref, *scratch):
    qs_refs, s_refs, m_refs, acc_refs = scratch[0:2], scratch[2:6], scratch[6:8], scratch[8:10]
    i = pl.program_id(2)
    t = ATT_TILE
    ch = Q_CHUNK
    per_g = t // ch
    n_chunks = DA_GROUP * per_g

    row = lax.broadcasted_iota(jnp.int32, (LANES, 1), 0)
    for g in range(DA_GROUP):
        qg = qt_ref[g * LANES:(g + 1) * LANES, :]
        qs_refs[0][:, g * t:(g + 1) * t] = jnp.where(row < DA_HEAD_DIM, qg, jnp.zeros_like(qg))
        qs_refs[1][:, g * t:(g + 1) * t] = jnp.where(row < DA_HEAD_DIM, jnp.zeros_like(qg), qg)
    for mp in range(2):
        m_refs[mp][...] = jnp.full(m_refs[mp].shape, MASKED, F32)
        acc_refs[mp][...] = jnp.zeros(acc_refs[mp].shape, F32)

    def n_keys(r, cc):
        return (cc + 1) * ch if isinstance(r, int) and r == 0 else t

    def stream(mp):
        def qk(r, buf):
            ks = pl.multiple_of((i - r) * t, t)
            for c in range(n_chunks):
                g, cc = divmod(c, per_g)
                nk = n_keys(r, cc)
                cols = slice(c * ch, (c + 1) * ch)
                s = _dot(k_ref[pl.ds(ks, nk), :], qs_refs[mp][:, cols])
                if isinstance(r, int) and r == 0:
                    s = s + bdiag_ref[g, 0:nk, cc * ch:(cc + 1) * ch]
                elif isinstance(r, int) and r == 1 and cc == 0:
                    s = s + bsub_ref[g]
                s_refs[2 * mp + buf][0:nk, cols] = s

        def absorb(r, buf):
            ks = pl.multiple_of((i - r) * t, t)
            chunks = []
            for c in range(n_chunks):
                nk = n_keys(r, c % per_g)
                cols = slice(c * ch, (c + 1) * ch)
                chunks.append((s_refs[2 * mp + buf].at[0:nk, cols], vt_ref[:, pl.ds(ks, nk)], cols))
            _absorb_chunks(chunks, m_refs[mp], acc_refs[mp])
        return qk, absorb

    qk0, absorb0 = stream(0)
    qk1, absorb1 = stream(1)
    _key_tile_pipeline(i, qk0, absorb0, beside_last=lambda: qk1(0, 0))
    _key_tile_pipeline(i, qk1, absorb1, scores_ready=True)

    o0, o1 = [a[0:DA_V_DIM, :] * (1.0 / a[DA_V_DIM:DA_V_DIM + 1, :]) for a in acc_refs]
    comb = o0 - lam_ref[0] * o1
    for g in range(DA_GROUP):
        o_ref[:, g * LANES:(g + 1) * LANES] = jnp.transpose(comb[:, g * t:(g + 1) * t])


def _diff_prompt(lam, qdt, kdb, vdt, bias_diag, bias_sub):
    b, _, s = qdt.shape
    t = ATT_TILE
    assert s % t == 0
    vrows = DA_V_DIM + SUM_ROWS
    return pl.pallas_call(
        _diff_prompt_body,
        grid=(b, DA_KV_HEADS, s // t),
        in_specs=[pl.BlockSpec(memory_space=pltpu.SMEM),
                  pl.BlockSpec((None, DA_GROUP * LANES, t), lambda bi, h, i: (bi, h, i)),
                  pl.BlockSpec((None, s, LANES), lambda bi, h, i: (bi, 0, h)),
                  pl.BlockSpec((None, vrows, s), lambda bi, h, i: (bi, h, 0)),
                  pl.BlockSpec((DA_GROUP, t, t), lambda bi, h, i: (h, 0, 0)),
                  pl.BlockSpec((DA_GROUP, t, Q_CHUNK), lambda bi, h, i: (h, 0, 0))],
        out_specs=pl.BlockSpec((None, t, DA_GROUP * LANES), lambda bi, h, i: (bi, i, h)),
        out_shape=jax.ShapeDtypeStruct((b, s, DA_WIDTH), F32),
        scratch_shapes=([pltpu.VMEM((LANES, DA_GROUP * t), BF16)] * 2
                        + [pltpu.VMEM((t, DA_GROUP * t), F32)] * 4
                        + [pltpu.VMEM((1, DA_GROUP * t), F32)] * 2
                        + [pltpu.VMEM((vrows, DA_GROUP * t), F32)] * 2),
        compiler_params=pltpu.CompilerParams(
            dimension_semantics=("parallel", "parallel", "arbitrary"), vmem_limit_bytes=VMEM_LIMIT_BYTES),
        name="diff_prompt",
    )(lam, qdt, kdb, vdt, bias_diag, bias_sub)


def _mla_prompt_body(qt_ref, k_ref, vt_ref, o_ref, *scratch):
    s_refs, m_refs, acc_refs = scratch[0:4], scratch[4:6], scratch[6:8]
    i = pl.program_id(2)
    t = ATT_TILE
    ch = Q_CHUNK
    n_chunks = t // ch
    vrows = MLA_V_DIM + SUM_ROWS
    scale2 = (MLA_QK_DIM ** -0.5) * LOG2E
    for hh in range(2):
        m_refs[hh][...] = jnp.full(m_refs[hh].shape, MASKED, F32)
        acc_refs[hh][...] = jnp.zeros(acc_refs[hh].shape, F32)

    def n_keys(r, c):
        return (c + 1) * ch if isinstance(r, int) and r == 0 else t

    def stream(hh):
        hs = slice(hh * LANES, (hh + 1) * LANES)
        vs = slice(hh * vrows, (hh + 1) * vrows)

        def qk(r, buf):
            ks = pl.multiple_of((i - r) * t, t)
            for c in range(n_chunks):
                nk = n_keys(r, c)
                cols = slice(c * ch, (c + 1) * ch)
                s = _dot(k_ref[pl.ds(ks, nk), hs], qt_ref[hs, cols]) * scale2
                if isinstance(r, int) and r == 0:
                    kidx = lax.broadcasted_iota(jnp.int32, (nk, ch), 0)
                    qidx = lax.broadcasted_iota(jnp.int32, (nk, ch), 1) + c * ch
                    s = jnp.where(kidx <= qidx, s, MASKED)
                s_refs[2 * hh + buf][0:nk, cols] = s

        def absorb(r, buf):
            ks = pl.multiple_of((i - r) * t, t)
            chunks = []
            for c in range(n_chunks):
                nk = n_keys(r, c)
                cols = slice(c * ch, (c + 1) * ch)
                chunks.append((s_refs[2 * hh + buf].at[0:nk, cols], vt_ref[vs, pl.ds(ks, nk)], cols))
            _absorb_chunks(chunks, m_refs[hh], acc_refs[hh])
        return qk, absorb

    qk0, absorb0 = stream(0)
    qk1, absorb1 = stream(1)
    _key_tile_pipeline(i, qk0, absorb0, beside_last=lambda: qk1(0, 0))
    _key_tile_pipeline(i, qk1, absorb1, scores_ready=True)

    outs = [a[0:MLA_V_DIM, :] * (1.0 / a[MLA_V_DIM:MLA_V_DIM + 1, :]) for a in acc_refs]
    o_ref[...] = jnp.transpose(jnp.concatenate(outs, axis=0))


def _mla_prompt(qmt, km, vmt):
    b, _, s = qmt.shape
    t = ATT_TILE
    vrows = MLA_V_DIM + SUM_ROWS
    return pl.pallas_call(
        _mla_prompt_body,
        grid=(b, MLA_HEADS // 2, s // t),
        in_specs=[pl.BlockSpec((None, 2 * LANES, t), lambda bi, hp, i: (bi, hp, i)),
                  pl.BlockSpec((None, s, 2 * LANES), lambda bi, hp, i: (bi, 0, hp)),
                  pl.BlockSpec((None, 2 * vrows, s), lambda bi, hp, i: (bi, hp, 0))],
        out_specs=pl.BlockSpec((None, t, 2 * MLA_V_DIM), lambda bi, hp, i: (bi, i, hp)),
        out_shape=jax.ShapeDtypeStruct((b, s, MLA_WIDTH), F32),
        scratch_shapes=([pltpu.VMEM((t, t), F32)] * 4
                        + [pltpu.VMEM((1, t), F32)] * 2
                        + [pltpu.VMEM((vrows, t), F32)] * 2),
        compiler_params=pltpu.CompilerParams(
            dimension_semantics=("parallel", "parallel", "arbitrary"), vmem_limit_bytes=VMEM_LIMIT_BYTES),
        name="mla_prompt",
    )(qmt, km, vmt)


def _sample_body(n_seq, n_steps, pt_ref, lam_ref, ck_hbm, cv_hbm, cc_hbm, cr_hbm,
                 kn_ref, vn_ref, cn_ref, rn_ref, qbd_ref, qmn_ref, qrr_ref, gkn_ref, wuk_ref, wuv_ref, eseg_ref,
                 bias_ref, mmask_ref, ctab_ref, stab_ref,
                 oa_ref, ob_ref,
                 qabs_ref, md_ref, ld_ref, accd_ref, mm_ref, lm_ref, accm_ref, kbuf, vbuf, cbuf, rbuf, sems):
    pg = PAGES_PER_STEP
    b = pl.program_id(0)
    j = pl.program_id(1)
    scale_m = (MLA_QK_DIM ** -0.5) * LOG2E
    page = cc_hbm.shape[1]
    krows = page * DA_KV_HEADS
    t_new = cn_ref.shape[0]
    n_mla = MLA_HEADS * t_new

    def page_copies(pid, p, slot):
        return (pltpu.make_async_copy(ck_hbm.at[pid], kbuf.at[slot, pl.ds(p * krows, krows), :], sems.at[slot, 0]),
                pltpu.make_async_copy(cv_hbm.at[pid], vbuf.at[slot, pl.ds(p * krows, krows), :], sems.at[slot, 1]),
                pltpu.make_async_copy(cc_hbm.at[pid], cbuf.at[slot, pl.ds(p * page, page), :], sems.at[slot, 2]),
                pltpu.make_async_copy(cr_hbm.at[pid], rbuf.at[slot, :, pl.ds(p * page, page)], sems.at[slot, 3]))

    def start_step(bb, jj, slot):
        for p in range(pg):
            for cp in page_copies(pt_ref[bb, jj * pg + p], p, slot):
                cp.start()

    def wait_step(slot):
        for p in range(pg):
            for cp in page_copies(0, p, slot):
                cp.wait()

    g = b * n_steps + j
    n_global = n_seq * n_steps
    slot = lax.rem(g, PAGE_SLOTS)
    ahead = PAGE_SLOTS - 1

    @pl.when(g == 0)
    def _():
        for s0 in range(min(ahead, n_global)):
            start_step(s0 // n_steps, s0 % n_steps, s0 % PAGE_SLOTS)

    @pl.when(g + ahead < n_global)
    def _():
        nxt = g + ahead
        start_step(nxt // n_steps, lax.rem(nxt, n_steps), lax.rem(nxt, PAGE_SLOTS))

    wait_step(slot)

    def heads_to_lanes(ref2, start=0, t=None):
        t = ref2.shape[0] // DA_KV_HEADS if t is None else t
        return jnp.concatenate([ref2[pl.ds(start + h, t, stride=DA_KV_HEADS), :].astype(BF16)
                                for h in range(DA_KV_HEADS)], axis=1)

    def rows_from_lanes(alpha, n_rows):
        full = jnp.broadcast_to(alpha, (LANES, alpha.shape[1]))
        if alpha.shape[1] < LANES:
            full = jnp.concatenate([full, jnp.zeros((LANES, LANES - alpha.shape[1]), F32)], axis=1)
        return jnp.transpose(full)[:n_rows, :]

    def score_matmuls(kb, c, krt, ctab, stab):
        cb = c.astype(BF16)
        s = _dot(kb, qbd_ref[...])
        kn = _dot(cb, wuk_ref[...])
        raw = _dot(cb, qabs_ref[...])
        ext = _dot_tn(jnp.concatenate([krt * ctab, krt * stab, krt * krt], axis=0).astype(BF16), qrr_ref[...])
        return cb, s, kn, raw, ext

    def local_softmaxes(blocks):
        half = MLA_HEADS * NOPE_HALF
        ss = []
        for (cb, s, kn, raw, ext), _, _, _ in blocks:
            kn2 = kn * kn
            ss.append(_dot((kn2[:, :half] + kn2[:, half:]).astype(BF16), eseg_ref[...]))
        sa = [s + bias for (_, s, _, _, _), _, bias, _ in blocks]
        ma = [jnp.max(s, axis=0, keepdims=True) for s in sa]
        pa = [jnp.exp2(s - m) for s, m in zip(sa, ma)]
        sb = []
        for ((_, _, _, raw, ext), _, _, mmask), q in zip(blocks, ss):
            r = lax.rsqrt((q + pltpu.roll(ext, LANES - n_mla, 1)[:, :n_mla]) / MLA_QK_DIM + EPS)
            sm = (raw + ext[:, :n_mla]) * r * scale_m
            sb.append(sm if mmask is None else sm + mmask)
        mb = [jnp.max(s, axis=0, keepdims=True) for s in sb]
        pb_ = [jnp.exp2(s - m) for s, m in zip(sb, mb)]
        parts_a = [(m, jnp.sum(p, axis=0, keepdims=True), _dot_tn(p.astype(BF16), vb))
                   for m, p, (_, vb, _, _) in zip(ma, pa, blocks)]
        parts_b = [(m, jnp.sum(p, axis=0, keepdims=True), _dot_tn(p.astype(BF16), pre[0]))
                   for m, p, (pre, _, _, _) in zip(mb, pb_, blocks)]
        return parts_a, parts_b

    def fold(m_ref, l_ref, acc_ref, parts):
        n_rows, width = acc_ref.shape
        m_old = m_ref[...]
        m_new = m_old
        for m, _, _ in parts:
            m_new = jnp.maximum(m_new, m)
        wide = lambda f: jnp.concatenate([rows_from_lanes(f, n_rows)] * (width // LANES), axis=1)
        alpha = jnp.exp2(m_old - m_new)
        l_new = alpha * l_ref[...]
        acc = wide(alpha) * acc_ref[...]
        for m, l, a in parts:
            f = jnp.exp2(m - m_new)
            l_new = l_new + f * l
            acc = acc + wide(f) * a
        m_ref[...] = m_new
        l_ref[...] = l_new
        acc_ref[...] = acc

    @pl.when(j == 0)
    def _():
        qg = (qmn_ref[...].astype(F32) * gkn_ref[...]).astype(BF16)
        qabs_ref[...] = _dot(wuk_ref[...], qg).astype(BF16)
        past = n_steps * pg * page
        rows_n = pl.ds(past, t_new)
        pre = score_matmuls(heads_to_lanes(kn_ref), cn_ref[...], rn_ref[...], ctab_ref[:, rows_n], stab_ref[:, rows_n])
        ((m_d, l_d, pv),), ((m_m, l_m, pc),) = local_softmaxes(
            [(pre, heads_to_lanes(vn_ref), bias_ref[rows_n, :], mmask_ref[...])])
        md_ref[...] = m_d
        ld_ref[...] = l_d
        accd_ref[...] = pv
        mm_ref[...] = m_m
        lm_ref[...] = l_m
        accm_ref[...] = pc

    base = j * (pg * page)
    pb = PAGES_PER_BLOCK
    blocks = []
    tb = pb * page
    for blk in range(pg // pb):
        rows = pl.ds(pl.multiple_of(base + blk * tb, tb), tb)
        keys = slice(blk * tb, (blk + 1) * tb)
        pre = score_matmuls(heads_to_lanes(kbuf.at[slot], blk * tb * DA_KV_HEADS, tb), cbuf[slot, keys, :],
                            rbuf[slot, :, keys], ctab_ref[:, rows], stab_ref[:, rows])
        blocks.append((pre, heads_to_lanes(vbuf.at[slot], blk * tb * DA_KV_HEADS, tb), bias_ref[rows, :], None))
    parts_d, parts_m = local_softmaxes(blocks)
    fold(md_ref, ld_ref, accd_ref, parts_d)
    fold(mm_ref, lm_ref, accm_ref, parts_m)

    @pl.when(j == n_steps - 1)
    def _():
        lam = lam_ref[0]
        inv_d = rows_from_lanes(1.0 / ld_ref[...], LANES)
        accd = accd_ref[...]
        lanes_per_kv = 2 * DA_GROUP * t_new
        for h in range(DA_KV_HEADS):
            cols = slice(h * LANES, (h + 1) * LANES)
            for g in range(DA_GROUP):
                r1 = h * lanes_per_kv + g * t_new
                r2 = r1 + DA_GROUP * t_new
                o1 = accd[r1:r1 + t_new, cols] * inv_d[r1:r1 + t_new, :]
                o2 = accd[r2:r2 + t_new, cols] * inv_d[r2:r2 + t_new, :]
                hq = h * DA_GROUP + g
                oa_ref[:, hq * LANES:(hq + 1) * LANES] = o1 - lam * o2
        inv_m = rows_from_lanes(1.0 / lm_ref[...], n_mla)
        cbar = (accm_ref[...] * jnp.concatenate([inv_m] * (MLA_KV_RANK // LANES), axis=1)).astype(BF16)
        full = _dot(cbar, wuv_ref[...])
        lane5 = lax.broadcasted_iota(jnp.int32, (1, MLA_WIDTH), 1)
        ob = jnp.zeros((t_new, MLA_WIDTH), F32)
        for h in range(MLA_HEADS):
            sel = (lane5 >= h * MLA_V_DIM) & (lane5 < (h + 1) * MLA_V_DIM)
            ob = ob + jnp.where(sel, full[h * t_new:(h + 1) * t_new, :], 0.0)
        ob_ref[...] = ob


def _sample_attention(lam, page_table, ck, cv, cc, crt, new, qbd, qmn, qrr, w, bias_tab, mla_mask, ctab, stab):
    n_seq, n_pages = page_table.shape
    pool, page = cc.shape[0], cc.shape[1]
    t_new = new['c'].shape[1]
    pg = PAGES_PER_STEP
    assert n_pages % pg == 0 and pg % PAGES_PER_BLOCK == 0 and MLA_HEADS * t_new <= LANES // 2
    n_steps = n_pages // pg

    def seq_spec(*tail):
        zeros = (0,) * len(tail)
        return pl.BlockSpec((None,) + tail, lambda b, j, pt: (b,) + zeros)

    def const(shape):
        nd = len(shape)
        return pl.BlockSpec(shape, lambda b, j, pt: (0,) * nd, pipeline_mode=pl.Buffered(1))

    in_specs = [pl.BlockSpec(memory_space=pltpu.SMEM)] + [pl.BlockSpec(memory_space=pl.ANY)] * 4
    per_seq = [new['kd'], new['vd'], new['c'], new['krt'], qbd, qmn, qrr]
    in_specs += [seq_spec(*a.shape[1:]) for a in per_seq]
    consts = [w['gk_nope_col'], w['w_uk'], w['w_uv'], w['eseg'], bias_tab, mla_mask, ctab, stab]
    in_specs += [const(a.shape) for a in consts]
    operands = [ck, cv, cc, crt] + per_seq + consts
    krows = page * DA_KV_HEADS
    grid_spec = pltpu.PrefetchScalarGridSpec(
        num_scalar_prefetch=1,
        grid=(n_seq, n_steps),
        in_specs=in_specs,
        out_specs=[pl.BlockSpec((None, t_new, DA_WIDTH), lambda b, j, pt: (b, 0, 0)),
                   pl.BlockSpec((None, t_new, MLA_WIDTH), lambda b, j, pt: (b, 0, 0))],
        scratch_shapes=[pltpu.VMEM((MLA_KV_RANK, MLA_HEADS * t_new), BF16),
                        pltpu.VMEM((1, LANES), F32), pltpu.VMEM((1, LANES), F32),
                        pltpu.VMEM((LANES, DA_KV_HEADS * DA_V_DIM), F32),
                        pltpu.VMEM((1, MLA_HEADS * t_new), F32), pltpu.VMEM((1, MLA_HEADS * t_new), F32),
                        pltpu.VMEM((MLA_HEADS * t_new, MLA_KV_RANK), F32),
                        pltpu.VMEM((PAGE_SLOTS, pg * krows, LANES), F32),
                        pltpu.VMEM((PAGE_SLOTS, pg * krows, DA_V_DIM), F32),
                        pltpu.VMEM((PAGE_SLOTS, pg * page, MLA_KV_RANK), F32),
                        pltpu.VMEM((PAGE_SLOTS, MLA_ROPE, pg * page), F32),
                        pltpu.SemaphoreType.DMA((PAGE_SLOTS, 4))])
    return pl.pallas_call(
        functools.partial(_sample_body, n_seq, n_steps),
        grid_spec=grid_spec,
        out_shape=[jax.ShapeDtypeStruct((n_seq, t_new, DA_WIDTH), F32),
                   jax.ShapeDtypeStruct((n_seq, t_new, MLA_WIDTH), F32)],
        compiler_params=pltpu.CompilerParams(dimension_semantics=("arbitrary", "arbitrary"),
                                             vmem_limit_bytes=VMEM_LIMIT_BYTES),
        name="sample_attention",
    )(page_table, lam, *operands)


def _merge_body(one_minus_lam_init, x_ref, oa_ref, ob_ref, ng_ref, wzg_ref, gout_ref, wpa_ref, wpb_ref, wo_ref, y_ref):
    x = x_ref[...]
    h = x * lax.rsqrt(jnp.mean(x * x, axis=-1, keepdims=True) + EPS) * ng_ref[...]
    hb = h.astype(BF16)
    za = _dot(hb, wzg_ref[:, 0:DA_WIDTH])
    gout = gout_ref[...]
    parts = []
    for hq in range(DA_HEADS):
        sl = slice(hq * LANES, (hq + 1) * LANES)
        o = oa_ref[:, sl]
        o = o * lax.rsqrt(jnp.mean(o * o, axis=-1, keepdims=True) + EPS) * gout * one_minus_lam_init
        z = za[:, sl]
        parts.append((o * (z * jax.nn.sigmoid(z))).astype(BF16))
    ya = _dot(jnp.concatenate(parts, axis=1), wpa_ref[...])
    zb = _dot(hb, wzg_ref[:, DA_WIDTH:DA_WIDTH + MLA_WIDTH])
    yb = _dot((ob_ref[...] * (zb * jax.nn.sigmoid(zb))).astype(BF16), wpb_ref[...])
    g0 = DA_WIDTH + MLA_WIDTH
    ga = _dot(hb, wzg_ref[:, g0:g0 + D_MODEL])
    gb = _dot(hb, wzg_ref[:, g0 + D_MODEL:g0 + 2 * D_MODEL])
    mix = jax.nn.sigmoid(ga) * ya + jax.nn.sigmoid(gb) * yb
    y_ref[...] = x + _dot(mix.astype(BF16), wo_ref[...])


def _merge(x2d, oa, ob, w, lam_init):
    n = x2d.shape[0]
    tm = min(TOKEN_TILE, n)
    assert n % tm == 0
    row = lambda cols: pl.BlockSpec((tm, cols), lambda i: (i, 0))
    consts = [w['norm_g'], w['w_zg'], w['gout'], w['w_pa'], w['w_pb'], w['w_o']]
    return pl.pallas_call(
        functools.partial(_merge_body, 1.0 - lam_init),
        grid=(n // tm,),
        in_specs=[row(D_MODEL), row(DA_WIDTH), row(MLA_WIDTH)] + [_const_spec(a.shape) for a in consts],
        out_specs=row(D_MODEL),
        out_shape=jax.ShapeDtypeStruct((n, D_MODEL), F32),
        compiler_params=pltpu.CompilerParams(dimension_semantics=("parallel",), vmem_limit_bytes=VMEM_LIMIT_BYTES),
        name="merge",
    )(x2d, oa, ob, *consts)


def _rel_bucket(dist):
    n = jnp.maximum(dist, 0)
    max_exact = REL_BUCKETS // 2
    nf = jnp.maximum(n, 1).astype(F32)
    large = max_exact + (jnp.log(nf / max_exact) / math.log(REL_MAX_DIST / max_exact)
                         * (REL_BUCKETS - max_exact)).astype(jnp.int32)
    large = jnp.minimum(large, REL_BUCKETS - 1)
    return jnp.where(n < max_exact, n, large)


def _rope_angles(pos):
    inv = jnp.power(ROPE_THETA, -jnp.arange(HALF, dtype=F32) / HALF)
    ang = pos.astype(F32)[:, None] * inv[None, :]
    return jnp.cos(ang), jnp.sin(ang)


def _rope_tables128(pos):
    cos, sin = _rope_angles(pos)
    n = pos.shape[0]
    cos_t = jnp.concatenate([jnp.ones((n, MLA_NOPE), F32), cos, cos, jnp.zeros((n, LANES - MLA_QK_DIM), F32)], axis=1)
    sin_t = jnp.concatenate([jnp.zeros((n, MLA_NOPE), F32), -sin, sin, jnp.zeros((n, LANES - MLA_QK_DIM), F32)], axis=1)
    return cos_t, sin_t


def _pad_heads(a, live):
    pad = [(0, 0)] * (a.ndim - 1) + [(0, LANES - live)]
    a = jnp.pad(a, pad)
    return a.reshape(a.shape[:-2] + (a.shape[-2] * LANES,))


def _prep_layer(p, t_new):
    offs = [int(v) for v in np.cumsum((0,) + IN_SPLITS)]
    wq, wk, wv, wza, wcq, wckv, wkr, wzb, wga, wgb = [p['w_in'][:, offs[n]:offs[n + 1]] for n in range(10)]
    with_x1_copy = lambda a: jnp.concatenate([a, a[..., MLA_NOPE:MLA_NOPE + HALF]], axis=-1)
    tail = LANES - MLA_QK_DIM - HALF
    wkr_placed = jnp.pad(jnp.concatenate([wkr, wkr[:, :HALF]], axis=1), ((0, 0), (ROPE_LO, tail)))
    halves = np.arange(LANES) // DA_HEAD_DIM
    w_ukv = p['w_ukv'].reshape(MLA_KV_RANK, MLA_HEADS, MLA_NOPE + MLA_V_DIM)
    w_uk = w_ukv[:, :, :MLA_NOPE]
    w_uk_halves = jnp.transpose(w_uk.reshape(MLA_KV_RANK, MLA_HEADS, 2, NOPE_HALF), (0, 2, 1, 3))
    gmk = p['mla_k_norm']
    gk_nope = jnp.broadcast_to(gmk[:MLA_NOPE].reshape(2, 1, NOPE_HALF), (2, MLA_HEADS, NOPE_HALF))
    head_of_row = np.arange(MLA_HEADS * NOPE_HALF) // NOPE_HALF
    head_of_lane = np.arange(MLA_HEADS * t_new) // t_new
    return dict(
        norm_g=p['norm_g'].reshape(1, D_MODEL),
        w_main=jnp.concatenate([wq, wk, wv, wcq, wckv, wkr_placed], axis=1).astype(BF16),
        w_zg=jnp.concatenate([wza, wzb, wga, wgb], axis=1).astype(BF16),
        gq=p['da_q_norm'].reshape(1, LANES) * ((DA_HEAD_DIM ** -0.5) * LOG2E),
        gk=p['da_k_norm'].reshape(1, LANES),
        gcq=p['mla_cq_norm'].reshape(1, MLA_Q_RANK),
        w_uq=_pad_heads(with_x1_copy(p['w_uq'].reshape(MLA_Q_RANK, MLA_HEADS, MLA_QK_DIM)),
                        MLA_QK_DIM + HALF).astype(BF16),
        gmq=jnp.pad(with_x1_copy(p['mla_q_norm']), (0, tail)).reshape(1, LANES),
        halves=jnp.asarray(halves[:, None] == halves[None, :], BF16),
        gckv=p['mla_ckv_norm'].reshape(1, MLA_KV_RANK),
        w_uk_pad=_pad_heads(w_uk, MLA_NOPE).astype(BF16),
        w_uk=w_uk_halves.reshape(MLA_KV_RANK, MLA_HEADS * MLA_NOPE).astype(BF16),
        w_uv=w_ukv[:, :, MLA_NOPE:].reshape(MLA_KV_RANK, MLA_WIDTH).astype(BF16),
        gmk=jnp.pad(with_x1_copy(gmk), (0, tail)).reshape(1, LANES),
        gk_nope_col=gk_nope.reshape(MLA_HEADS * MLA_NOPE, 1),
        gk_rope=gmk[MLA_NOPE:],
        eseg=jnp.asarray(head_of_row[:, None] == head_of_lane[None, :], BF16),
        gout=p['da_out_norm'].reshape(1, DA_V_DIM),
        w_pa=p['w_pa'].astype(BF16), w_pb=p['w_pb'].astype(BF16), w_o=p['w_o'].astype(BF16),
    )


def _toeplitz(v, n):
    lead = v.shape[:-1]
    u = jnp.flip(v, axis=-1)
    w = jnp.concatenate([u, jnp.zeros(lead + (1,), v.dtype)], axis=-1)
    flat = jnp.tile(w, (1,) * len(lead) + (n,))[..., :n * (2 * n - 1)]
    return flat.reshape(lead + (n, 2 * n - 1))[..., n - 1:]


def _prompt_bias_tiles(rel_bias):
    t = ATT_TILE
    assert REL_MAX_DIST == LANES
    far = rel_bias[REL_BUCKETS - 1]
    d0 = jnp.arange(2 * LANES - 1, dtype=jnp.int32) - (LANES - 1)
    f0 = jnp.where((d0 >= 0)[:, None], rel_bias[_rel_bucket(d0)] - far[None, :], MASKED)
    on_diag = jnp.swapaxes(_toeplitz(f0.T, LANES), 1, 2) * LOG2E
    below = jnp.swapaxes(_toeplitz((rel_bias[_rel_bucket(d0 + LANES)] - far[None, :]).T, LANES), 1, 2) * LOG2E
    masked = jnp.full_like(on_diag, MASKED)
    zero = jnp.zeros_like(on_diag)
    nb = t // LANES
    diag = jnp.concatenate(
        [jnp.concatenate([masked if qb < kb else on_diag if qb == kb else below if qb == kb + 1 else zero
                          for qb in range(nb)], axis=2) for kb in range(nb)], axis=1)
    sub = jnp.pad(below, ((0, 0), (t - LANES, 0), (0, Q_CHUNK - LANES)))
    return diag, sub


def _sample_tables(rel_bias, gk_rope, past, t_new):
    n_keys = past + t_new
    near = REL_MAX_DIST + t_new
    k_pos = jnp.arange(n_keys - near, n_keys, dtype=jnp.int32)
    q_pos = past + jnp.arange(t_new, dtype=jnp.int32)
    b = rel_bias[_rel_bucket(q_pos[None, :] - k_pos[:, None])]
    b = b - rel_bias[REL_BUCKETS - 1][None, None, :]
    b = jnp.where((q_pos[None, :] >= k_pos[:, None])[:, :, None], b, MASKED)
    b = jnp.transpose(b.reshape(near, t_new, DA_KV_HEADS, DA_GROUP), (0, 2, 3, 1))
    b = jnp.broadcast_to(b[:, :, None], (near, DA_KV_HEADS, 2, DA_GROUP, t_new)).reshape(near, 2 * DA_HEADS * t_new)
    b = jnp.pad(b.astype(F32) * LOG2E, ((n_keys - near, 0), (0, 0)))
    newk = jnp.arange(t_new)
    mm = jnp.where(newk[None, :] >= newk[:, None], 0.0, MASKED).astype(F32)
    mm = jnp.tile(mm, (1, MLA_HEADS))
    cos, sin = _rope_angles(jnp.arange(n_keys, dtype=jnp.int32))
    ctab = jnp.concatenate([cos, cos], axis=1) * gk_rope[None, :]
    stab = jnp.concatenate([sin, sin], axis=1) * gk_rope[None, :]
    return b, mm, ctab.T, stab.T


def _sample_query_layouts(qd, qm, n_seq, t_new):
    q = qd.reshape(n_seq, t_new, DA_KV_HEADS, DA_GROUP, 2, DA_HEAD_DIM)
    n_blk = DA_KV_HEADS * 2
    q = jnp.transpose(q, (0, 2, 4, 5, 3, 1)).reshape(n_seq, n_blk * DA_HEAD_DIM, DA_GROUP * t_new)
    own = (np.arange(n_blk * DA_HEAD_DIM)[:, None] // DA_HEAD_DIM) == (np.arange(n_blk * DA_GROUP * t_new)[None, :]
                                                                      // (DA_GROUP * t_new))
    qbd = jnp.where(own[None], jnp.tile(q, (1, 1, n_blk)), jnp.zeros((), qd.dtype))
    qh = qm.reshape(n_seq, t_new, MLA_HEADS, LANES)
    qn = jnp.transpose(qh[..., :MLA_NOPE].reshape(n_seq, t_new, MLA_HEADS, 2, NOPE_HALF), (0, 3, 2, 4, 1))
    qn = qn.reshape(n_seq, MLA_HEADS * MLA_NOPE, t_new)
    row_head = (np.arange(MLA_HEADS * MLA_NOPE) // NOPE_HALF) % MLA_HEADS
    own_h = row_head[:, None] == (np.arange(MLA_HEADS * t_new)[None, :] // t_new)
    qmn = jnp.where(own_h[None], jnp.tile(qn, (1, 1, MLA_HEADS)), jnp.zeros((), qm.dtype))
    qr = jnp.transpose(qh[..., ROPE_LO:ROPE_LO + MLA_ROPE], (0, 3, 2, 1)).reshape(n_seq, MLA_ROPE, MLA_HEADS * t_new)
    qrs = jnp.concatenate([qr[:, HALF:], -qr[:, :HALF]], axis=1)
    n_mla = MLA_HEADS * t_new
    top = jnp.pad(jnp.concatenate([qr, qrs], axis=1), ((0, 0), (0, 0), (0, LANES - n_mla)))
    bottom = jnp.pad(jnp.ones((n_seq, MLA_ROPE, n_mla), qm.dtype), ((0, 0), (0, 0), (LANES - n_mla, 0)))
    return qbd, qmn, jnp.concatenate([top, bottom], axis=1)


def kernel(x_prompt, x_sample, cache_diff_k, cache_diff_v, cache_mla_latent, cache_mla_krope, page_table, rel_bias,
           norm_g, w_in, da_q_norm, da_k_norm, lam_q1, lam_k1, lam_q2, lam_k2, da_out_norm, w_pa, mla_cq_norm, w_uq,
           mla_ckv_norm, w_ukv, mla_q_norm, mla_k_norm, w_pb, w_o):
    b, s, _ = x_prompt.shape
    n_seq, t_new, _ = x_sample.shape
    depth, pool, page = cache_diff_k.shape[:3]
    past = page_table.shape[1] * page
    page_table = page_table.astype(jnp.int32)

    cos_p, sin_p = _rope_tables128(jnp.arange(s, dtype=jnp.int32))
    rows_s = min(TOKEN_TILE, n_seq * t_new)
    cos_s, sin_s = _rope_tables128(past + (jnp.arange(rows_s, dtype=jnp.int32) % t_new))
    bias_diag, bias_sub = _prompt_bias_tiles(rel_bias)

    xp = x_prompt.reshape(b * s, D_MODEL)
    xs = x_sample.reshape(n_seq * t_new, D_MODEL)
    news = [[] for _ in range(8)]
    for l in range(depth):
        p = dict(norm_g=norm_g[l], w_in=w_in[l], da_q_norm=da_q_norm[l], da_k_norm=da_k_norm[l],
                 da_out_norm=da_out_norm[l], w_pa=w_pa[l], mla_cq_norm=mla_cq_norm[l], w_uq=w_uq[l],
                 mla_ckv_norm=mla_ckv_norm[l], w_ukv=w_ukv[l], mla_q_norm=mla_q_norm[l],
                 mla_k_norm=mla_k_norm[l], w_pb=w_pb[l], w_o=w_o[l])
        w = _prep_layer(p, t_new)
        lam_init = 0.8 - 0.6 * math.exp(-0.3 * l)
        lam = (jnp.exp(jnp.sum(lam_q1[l] * lam_k1[l]).astype(F32))
               - jnp.exp(jnp.sum(lam_q2[l] * lam_k2[l]).astype(F32)) + lam_init).reshape(1)

        qdt, kd, kdb, vd, vdt, c, kr, qmt, km, vmt = _project(xp, w, cos_p, sin_p, True, batch=b)
        oa_p = _diff_prompt(lam, qdt, kdb.reshape(b, s, 512), vdt, bias_diag, bias_sub)
        ob_p = _mla_prompt(qmt, km.reshape(b, s, MLA_HEADS * LANES), vmt)

        qd_s, kd_s, vd_s, c_s, kr_s, qm_s = _project(xs, w, cos_s, sin_s, False)
        qbd, qmn, qrr = _sample_query_layouts(qd_s, qm_s, n_seq, t_new)
        bias_tab, mla_mask, ctab, stab = _sample_tables(rel_bias, w['gk_rope'], past, t_new)
        new = dict(kd=kd_s.reshape(n_seq, t_new * DA_KV_HEADS, LANES), vd=vd_s.reshape(n_seq, t_new * DA_KV_HEADS, DA_V_DIM),
                   c=c_s.reshape(n_seq, t_new, MLA_KV_RANK),
                   krt=jnp.swapaxes(kr_s.reshape(n_seq, t_new, MLA_ROPE), 1, 2))
        oa_s, ob_s = _sample_attention(
            lam, page_table, cache_diff_k[l].reshape(pool, page * DA_KV_HEADS, LANES),
            cache_diff_v[l].reshape(pool, page * DA_KV_HEADS, DA_V_DIM), cache_mla_latent[l],
            jnp.swapaxes(cache_mla_krope[l], 1, 2), new, qbd, qmn, qrr, w, bias_tab, mla_mask, ctab, stab)

        xp = _merge(xp, oa_p.reshape(b * s, DA_WIDTH), ob_p.reshape(b * s, MLA_WIDTH), w, lam_init)
        xs = _merge(xs, oa_s.reshape(n_seq * t_new, DA_WIDTH), ob_s.reshape(n_seq * t_new, MLA_WIDTH), w, lam_init)

        for lst, a in zip(news, (kd.reshape(b, s, DA_KV_HEADS, LANES), vd.reshape(b, s, DA_KV_HEADS, DA_V_DIM),
                                 c.reshape(b, s, MLA_KV_RANK), kr.reshape(b, s, MLA_ROPE),
                                 kd_s.reshape(n_seq, t_new, DA_KV_HEADS, LANES),
                                 vd_s.reshape(n_seq, t_new, DA_KV_HEADS, DA_V_DIM),
                                 c_s.reshape(n_seq, t_new, MLA_KV_RANK), kr_s.reshape(n_seq, t_new, MLA_ROPE))):
            lst.append(a)
    return (xp.reshape(b, s, D_MODEL), xs.reshape(n_seq, t_new, D_MODEL)) + tuple(jnp.stack(a) for a in news)
```
